```python
import math
import jax, jax.numpy as jnp
from jax import lax
import numpy as np

D_MODEL = 1024
BATCH = 2
SEQ = 8192
DEPTH = 4

GRID_W = 64
CTX_LEN = 256
EPS = 1e-6

GDN_QK_HEADS = 8
GDN_V_HEADS = 16
GDN_HEAD_DIM = 128
GDN_QK_W = GDN_QK_HEADS * GDN_HEAD_DIM
GDN_V_W = GDN_V_HEADS * GDN_HEAD_DIM
GDN_QKV_W = 2 * GDN_QK_W + GDN_V_W
GDN_IN_W = GDN_QKV_W + GDN_V_W + 4 * GDN_V_HEADS
GDN_CHUNK = 64

LRU_W = D_MODEL
LRU_BLOCKS = 4
LRU_BS = LRU_W // LRU_BLOCKS
RG_C = 8.0

CONV_W = 4
CONV_LEFT = 2

N_GDN = (DEPTH + 1) // 2
N_LRU = DEPTH // 2

kernel_name = "hybrid_gdn_rglru_prefix_dit"


def rmsnorm(x, g):
    xf = x.astype(jnp.float32)
    y = xf * lax.rsqrt(jnp.mean(xf * xf, axis=-1, keepdims=True) + EPS)
    return y.astype(x.dtype) * g


def l2norm(x):
    xf = x.astype(jnp.float32)
    return (xf * lax.rsqrt(jnp.sum(xf * xf, axis=-1, keepdims=True) + EPS)).astype(x.dtype)


def short_conv(x, w):
    T = x.shape[1]
    xp = jnp.pad(x, ((0, 0), (CONV_LEFT, CONV_W - 1 - CONV_LEFT), (0, 0)))
    out = xp[:, 0:T] * w[0]
    for j in range(1, CONV_W):
        out = out + xp[:, j:j + T] * w[j]
    return out


def to_col_major(t, rows):
    B, T, D = t.shape
    return t.reshape(B, rows, GRID_W, D).transpose(0, 2, 1, 3).reshape(B, T, D)


def from_col_major(t, rows):
    B, T, D = t.shape
    return t.reshape(B, GRID_W, rows, D).transpose(0, 2, 1, 3).reshape(B, T, D)


def chunk_gated_delta(q, k, v, g, beta, s0):
    out_dtype = v.dtype
    q, k, v, g, beta = (t.astype(jnp.float32) for t in (q, k, v, g, beta))
    B, T, H, DK = q.shape
    DV = v.shape[-1]
    C = GDN_CHUNK
    N = T // C
    ch = lambda t: t.reshape(B, N, C, H, -1).transpose(0, 3, 1, 2, 4)
    q, k, v = ch(q), ch(k), ch(v)
    g = g.reshape(B, N, C, H).transpose(0, 3, 1, 2)
    beta = beta.reshape(B, N, C, H).transpose(0, 3, 1, 2)
    g = jnp.cumsum(g, axis=-1)
    tril = jnp.tril(jnp.ones((C, C), bool))
    strict = jnp.tril(jnp.ones((C, C), bool), -1)
    diff = g[..., :, None] - g[..., None, :]
    decay = jnp.where(tril, jnp.exp(jnp.where(tril, diff, 0.0)), 0.0)
    kb = k * beta[..., None]
    L = jnp.where(strict, jnp.einsum('bhncd,bhnsd->bhncs', kb, k) * decay, 0.0)
    eye = jnp.eye(C, dtype=jnp.float32)
    rhs = jnp.concatenate([v * beta[..., None], kb * jnp.exp(g)[..., None]], axis=-1)
    sol = lax.linalg.triangular_solve(eye + L, rhs, left_side=True, lower=True, unit_diagonal=True)
    u, w = sol[..., :DV], sol[..., DV:]
    qk = jnp.where(tril, jnp.einsum('bhncd,bhnsd->bhncs', q, k) * decay, 0.0)
    q_dec = q * jnp.exp(g)[..., None]
    k_dec = k * jnp.exp(g[..., -1:] - g)[..., None]
    g_last = jnp.exp(g[..., -1])

    def step(S, xs):
        qk_i, u_i, w_i, qd_i, kd_i, gl_i = xs
        v_new = u_i - jnp.einsum('bhcd,bhde->bhce', w_i, S)
        o = jnp.einsum('bhcd,bhde->bhce', qd_i, S) + jnp.einsum('bhcs,bhse->bhce', qk_i, v_new)
        S = S * gl_i[..., None, None] + jnp.einsum('bhcd,bhce->bhde', kd_i, v_new)
        return S, o

    xs = tuple(jnp.moveaxis(t, 2, 0) for t in (qk, u, w, q_dec, k_dec, g_last))
    S, o = lax.scan(step, s0.astype(jnp.float32), xs)
    o = o.transpose(1, 0, 3, 2, 4).reshape(B, T, H, DV)
    return o.astype(out_dtype), S


def gdn_sequence(h, w_in, conv_w, a_log, dt_bias, norm_g, w_out, s0):
    B, T, _ = h.shape
    proj = h @ w_in
    qkv, z, ab = jnp.split(proj, [GDN_QKV_W, GDN_QKV_W + GDN_V_W], axis=-1)
    qkv = jax.nn.silu(short_conv(qkv, conv_w))
    q, k, v = jnp.split(qkv, [GDN_QK_W, 2 * GDN_QK_W], axis=-1)
    rep = GDN_V_HEADS // GDN_QK_HEADS
    q = jnp.repeat(l2norm(q.reshape(B, T, GDN_QK_HEADS, GDN_HEAD_DIM)) * (GDN_HEAD_DIM ** -0.5), rep, axis=2)
    k = jnp.repeat(l2norm(k.reshape(B, T, GDN_QK_HEADS, GDN_HEAD_DIM)), rep, axis=2)
    v = v.reshape(B, T, GDN_V_HEADS, GDN_HEAD_DIM)
    ab = ab.reshape(B, T, 2, 2, GDN_V_HEADS).astype(jnp.float32)
    g = -jnp.exp(a_log) * jax.nn.softplus(ab[..., 0, :] + dt_bias)
    beta = jax.nn.sigmoid(ab[..., 1, :])
    o_f, s_f = chunk_gated_delta(q, k, v, g[:, :, 0], beta[:, :, 0], s0[0])
    fl = lambda t: jnp.flip(t, axis=1)
    o_b, s_b = chunk_gated_delta(fl(q), fl(k), fl(v), fl(g[:, :, 1]), fl(beta[:, :, 1]), s0[1])
    o = o_f + fl(o_b)
    o = rmsnorm(o, norm_g) * jax.nn.silu(z).reshape(B, T, GDN_V_HEADS, GDN_HEAD_DIM)
    y = o.reshape(B, T, GDN_V_W) @ w_out
    return y, jnp.stack([s_f, s_b])


def gdn_mixer(h_ctx, h_lat, w_in, conv_w, a_log, dt_bias, norm_g, w_out):
    B = h_ctx.shape[0]
    s0 = jnp.zeros((2, B, GDN_V_HEADS, GDN_HEAD_DIM, GDN_HEAD_DIM), jnp.float32)
    y_ctx, s_ctx = gdn_sequence(h_ctx, w_in, conv_w, a_log, dt_bias, norm_g, w_out, s0)
    y_lat, _ = gdn_sequence(h_lat, w_in, conv_w, a_log, dt_bias, norm_g, w_out, s_ctx)
    return y_ctx, y_lat


def linear_scan(a, b, h0):
    def combine(l, r):
        return (l[0] * r[0], r[0] * l[1] + r[1])
    a_cum, h = lax.associative_scan(combine, (a, b), axis=1)
    return a_cum * h0[:, None] + h


def rglru_sequence(h, w_in, conv_w, conv_b, w_r, b_r, w_i, b_i, lam, w_out, h0):
    B, T, _ = h.shape
    xb, gate = jnp.split(h @ w_in, [LRU_W], axis=-1)
    xc = short_conv(xb, conv_w) + conv_b
    xblk = xc.reshape(B, T, LRU_BLOCKS, LRU_BS)
    r = jax.nn.sigmoid(jnp.einsum('btnc,dncs->btdns', xblk, w_r).reshape(B, T, 2, LRU_W) + b_r)
    i = jax.nn.sigmoid(jnp.einsum('btnc,dncs->btdns', xblk, w_i).reshape(B, T, 2, LRU_W) + b_i)
    log_a = -RG_C * r.astype(jnp.float32) * jax.nn.softplus(-lam.astype(jnp.float32))
    a = jnp.exp(log_a)
    b = jnp.sqrt(-jnp.expm1(2.0 * log_a)) * (i.astype(jnp.float32) * xc.astype(jnp.float32)[:, :, None])
    h_f = linear_scan(a[:, :, 0], b[:, :, 0], h0[0])
    h_b = jnp.flip(linear_scan(jnp.flip(a[:, :, 1], 1), jnp.flip(b[:, :, 1], 1), h0[1]), 1)
    y = ((h_f + h_b).astype(h.dtype) * jax.nn.silu(gate)) @ w_out
    return y, jnp.stack([h_f[:, -1], h_b[:, 0]])


def rglru_mixer(h_ctx, h_lat, w_in, conv_w, conv_b, w_r, b_r, w_i, b_i, lam, w_out):
    B = h_ctx.shape[0]
    h0 = jnp.zeros((2, B, LRU_W), jnp.float32)
    y_ctx, h_ctx_state = rglru_sequence(h_ctx, w_in, conv_w, conv_b, w_r, b_r, w_i, b_i, lam, w_out, h0)
    y_lat, _ = rglru_sequence(h_lat, w_in, conv_w, conv_b, w_r, b_r, w_i, b_i, lam, w_out, h_ctx_state)
    return y_ctx, y_lat


def setup_inputs(seed: int = 0) -> dict:
    key = jax.random.key(seed)
    ks = jax.random.split(key, 24)
    nrm = lambda k, shape, s: jax.random.normal(k, shape, jnp.float32) * s
    x = nrm(ks[0], (BATCH, SEQ, D_MODEL), 1.0)
    c = nrm(ks[1], (BATCH, D_MODEL), 1.0)
    ctx = nrm(ks[2], (BATCH, CTX_LEN, D_MODEL), 1.0)
    c_ctx = nrm(ks[3], (D_MODEL,), 1.0)
    mod_w = nrm(ks[4], (DEPTH, D_MODEL, 3 * D_MODEL), 0.5 * D_MODEL ** -0.5)
    mod_b = nrm(ks[5], (DEPTH, 3 * D_MODEL), 0.02)
    norm_g = 1.0 + nrm(ks[6], (DEPTH, D_MODEL), 0.02)
    gdn_w_in = nrm(ks[7], (N_GDN, D_MODEL, GDN_IN_W), D_MODEL ** -0.5)
    gdn_conv = nrm(ks[8], (N_GDN, CONV_W, GDN_QKV_W), CONV_W ** -0.5)
    gdn_a_log = jnp.log(jax.random.uniform(ks[9], (N_GDN, 2, GDN_V_HEADS), jnp.float32, 1.0, 16.0))
    dt = jnp.exp(jax.random.uniform(ks[10], (N_GDN, 2, GDN_V_HEADS), jnp.float32, math.log(1e-3), math.log(1e-1)))
    gdn_dt_bias = dt + jnp.log(-jnp.expm1(-dt))
    gdn_norm_g = 1.0 + nrm(ks[11], (N_GDN, GDN_HEAD_DIM), 0.02)
    gdn_w_out = nrm(ks[12], (N_GDN, GDN_V_W, D_MODEL), GDN_V_W ** -0.5)
    lru_w_in = nrm(ks[13], (N_LRU, D_MODEL, 2 * LRU_W), D_MODEL ** -0.5)
    lru_conv_w = nrm(ks[14], (N_LRU, CONV_W, LRU_W), CONV_W ** -0.5)
    lru_conv_b = nrm(ks[15], (N_LRU, LRU_W), 0.02)
    lru_w_r = nrm(ks[16], (N_LRU, 2, LRU_BLOCKS, LRU_BS, LRU_BS), LRU_BS ** -0.5)
    lru_b_r = nrm(ks[17], (N_LRU, 2, LRU_W), 0.02)
    lru_w_i = nrm(ks[18], (N_LRU, 2, LRU_BLOCKS, LRU_BS, LRU_BS), LRU_BS ** -0.5)
    lru_b_i = nrm(ks[19], (N_LRU, 2, LRU_W), 0.02)
    a0 = jax.random.uniform(ks[20], (N_LRU, 2, LRU_W), jnp.float32, 0.9, 0.999)
    s = a0 ** (1.0 / RG_C)
    lru_lambda = jnp.log(s) - jnp.log1p(-s)
    lru_w_out = nrm(ks[21], (N_LRU, LRU_W, D_MODEL), LRU_W ** -0.5)
    final_g = 1.0 + nrm(ks[22], (D_MODEL,), 0.02)
    return {"x": x, "c": c, "ctx": ctx, "c_ctx": c_ctx,
            "mod_w": mod_w, "mod_b": mod_b, "norm_g": norm_g,
            "gdn_w_in": gdn_w_in, "gdn_conv": gdn_conv, "gdn_a_log": gdn_a_log,
            "gdn_dt_bias": gdn_dt_bias, "gdn_norm_g": gdn_norm_g, "gdn_w_out": gdn_w_out,
            "lru_w_in": lru_w_in, "lru_conv_w": lru_conv_w, "lru_conv_b": lru_conv_b,
            "lru_w_r": lru_w_r, "lru_b_r": lru_b_r, "lru_w_i": lru_w_i, "lru_b_i": lru_b_i,
            "lru_lambda": lru_lambda, "lru_w_out": lru_w_out, "final_g": final_g}


def reference(x, c, ctx, c_ctx, mod_w, mod_b, norm_g, gdn_w_in, gdn_conv, gdn_a_log, gdn_dt_bias,
              gdn_norm_g, gdn_w_out, lru_w_in, lru_conv_w, lru_conv_b, lru_w_r, lru_b_r, lru_w_i, lru_b_i,
              lru_lambda, lru_w_out, final_g):
    rows = x.shape[1] // GRID_W
    sc = jax.nn.silu(c)
    sc_ctx = jax.nn.silu(c_ctx)
    for i in range(DEPTH):
        shift, scale, gate = jnp.split(sc @ mod_w[i] + mod_b[i], 3, axis=-1)
        shift_c, scale_c, gate_c = jnp.split(sc_ctx @ mod_w[i] + mod_b[i], 3, axis=-1)
        h_lat = rmsnorm(x, norm_g[i]) * (1.0 + scale[:, None]) + shift[:, None]
        h_ctx = rmsnorm(ctx, norm_g[i]) * (1.0 + scale_c) + shift_c
        col_major = (i + i // 2) % 2 == 1
        if col_major:
            h_lat = to_col_major(h_lat, rows)
        j = i // 2
        if i % 2 == 0:
            y_ctx, y_lat = gdn_mixer(h_ctx, h_lat, gdn_w_in[j], gdn_conv[j], gdn_a_log[j],
                                     gdn_dt_bias[j], gdn_norm_g[j], gdn_w_out[j])
        else:
            y_ctx, y_lat = rglru_mixer(h_ctx, h_lat, lru_w_in[j], lru_conv_w[j], lru_conv_b[j],
                                       lru_w_r[j], lru_b_r[j], lru_w_i[j], lru_b_i[j],
                                       lru_lambda[j], lru_w_out[j])
        if col_major:
            y_lat = from_col_major(y_lat, rows)
        x = x + gate[:, None] * y_lat
        if i < DEPTH - 1:
            ctx = ctx + gate_c * y_ctx
    return rmsnorm(x, final_g)
```

```python
import functools

import jax
import jax.numpy as jnp
from jax import lax
from jax.experimental import pallas as pl
from jax.experimental.pallas import tpu as pltpu

F32 = jnp.float32
BF16 = jnp.bfloat16

EPS = 1e-6
GRID_W = 64
HEAD_DIM = 128
QK_HEADS = 8
V_HEADS = 16
CHUNK = 64
CONV_W = 4
CONV_LEFT = 2
HALO = 8
LRU_BLOCKS = 4
RG_C = 8.0
LANES = 128
MOD_ROWS = 8

VMEM_LIMIT = 56 * 1024 * 1024


def _cparams(sem):
    return pltpu.CompilerParams(dimension_semantics=sem, vmem_limit_bytes=VMEM_LIMIT)


def _pick(total, target, mult):
    best = None
    for cand in range(mult, min(total, target) + 1, mult):
        if total % cand == 0:
            best = cand
    assert best is not None, (total, target, mult)
    return best


def _sigmoid(x):
    return 1.0 / (1.0 + jnp.exp(-x))


def _softplus(x):
    return jnp.maximum(x, 0.0) + jnp.log(1.0 + jnp.exp(-jnp.abs(x)))


def _dot(a, b):
    return jnp.dot(a, b, preferred_element_type=F32)


def _dot_nt(a, b):
    return lax.dot_general(a, b, (((1,), (1,)), ((), ())), preferred_element_type=F32)


def _dot_tn(a, b):
    return lax.dot_general(a, b, (((0,), (0,)), ((), ())), preferred_element_type=F32)


def _mod_kernel(c_ref, w_ref, b_ref, o_ref):
    c = c_ref[...]
    sc = c * _sigmoid(c)
    o_ref[...] = jnp.dot(sc, w_ref[...], preferred_element_type=F32,
                         precision=lax.Precision.HIGHEST) + b_ref[...]


def _modulation(cond, mod_w, mod_b):
    depth, d, n3 = mod_w.shape
    return pl.pallas_call(
        _mod_kernel,
        grid=(depth, n3 // d),
        in_specs=[pl.BlockSpec((MOD_ROWS, d), lambda i, j: (0, 0)),
                  pl.BlockSpec((None, d, d), lambda i, j: (i, 0, j)),
                  pl.BlockSpec((None, 1, d), lambda i, j: (i, 0, j))],
        out_specs=pl.BlockSpec((None, MOD_ROWS, d), lambda i, j: (i, 0, j)),
        out_shape=jax.ShapeDtypeStruct((depth, MOD_ROWS, n3), F32),
        compiler_params=_cparams(("parallel", "parallel")),
        name="modulation",
    )(cond, mod_w, mod_b.reshape(depth, 1, n3))


def _stream_tiling(t, d, col_major, target):
    if not col_major:
        tm = _pick(t, target, CHUNK)
        return tm, 1, (lambda a: a), (None, tm, d), (lambda b, i: (b, i, 0))
    rows = t // GRID_W
    assert rows * GRID_W == t and rows % 8 == 0
    nw = _pick(GRID_W, max(target // rows, 1), 1)
    tm = rows * nw
    return (tm, nw, (lambda a: a.reshape(a.shape[0], rows, GRID_W * d)),
            (None, rows, nw * d), (lambda b, i: (b, 0, i)))


def _load_stream_tile(x_ref, nw, d):
    if nw == 1:
        return x_ref[...]
    return jnp.concatenate([x_ref[:, k * d:(k + 1) * d] for k in range(nw)], axis=0)


def _in_proj_kernel(*refs, nw, d, has_extra):
    if has_extra:
        x_ref, g_ref, sh_ref, sc_ref, w_ref, we_ref, o_ref, oe_ref, h_scr = refs
    else:
        x_ref, g_ref, sh_ref, sc_ref, w_ref, o_ref, h_scr = refs

    @pl.when(pl.program_id(2) == 0)
    def _():
        x = _load_stream_tile(x_ref, nw, d)
        y = x * lax.rsqrt(jnp.mean(x * x, axis=-1, keepdims=True) + EPS) * g_ref[...]
        h = (y * (1.0 + sc_ref[...]) + sh_ref[...]).astype(BF16)
        h_scr[...] = h
        if has_extra:
            oe_ref[...] = _dot(h, we_ref[...])

    o_ref[...] = _dot(h_scr[...], w_ref[...])


def _in_proj(x, norm_g, shift, scale, w, w_extra, *, col_major):
    b, t, d = x.shape
    n = w.shape[1]
    tn = _pick(n, 1024, LANES)
    tm, nw, view, xblock, xmap = _stream_tiling(t, d, col_major, 1024)
    has_extra = w_extra is not None
    vec = pl.BlockSpec((None, 1, d), lambda bi, i, j: (bi, 0, 0))
    in_specs = [pl.BlockSpec(xblock, lambda bi, i, j: xmap(bi, i)),
                pl.BlockSpec((1, d), lambda bi, i, j: (0, 0)), vec, vec,
                pl.BlockSpec((d, tn), lambda bi, i, j: (0, j))]
    out_specs = [pl.BlockSpec((None, tm, tn), lambda bi, i, j: (bi, i, j))]
    out_shape = [jax.ShapeDtypeStruct((b, t, n), F32)]
    args = [view(x), norm_g.reshape(1, d), shift, scale, w]
    if has_extra:
        in_specs.append(pl.BlockSpec((d, LANES), lambda bi, i, j: (0, 0)))
        out_specs.append(pl.BlockSpec((None, tm, LANES), lambda bi, i, j: (bi, i, 0)))
        out_shape.append(jax.ShapeDtypeStruct((b, t, LANES), F32))
        args.append(w_extra)
    outs = pl.pallas_call(
        functools.partial(_in_proj_kernel, nw=nw, d=d, has_extra=has_extra),
        grid=(b, t // tm, n // tn),
        in_specs=in_specs, out_specs=out_specs, out_shape=out_shape,
        scratch_shapes=[pltpu.VMEM((tm, d), BF16)],
        compiler_params=_cparams(("parallel", "parallel", "arbitrary")),
        name="in_proj",
    )(*args)
    return outs if has_extra else outs[0]


def _gdn_gate_kernel(ab_ref, alog_ref, dtb_ref, o_ref, *, tm):
    x = ab_ref[...]
    g = -jnp.exp(alog_ref[...]) * _softplus(x + dtb_ref[...])
    beta = _sigmoid(x)
    ri = lax.broadcasted_iota(jnp.int32, (CHUNK, CHUNK), 0)
    ci = lax.broadcasted_iota(jnp.int32, (CHUNK, CHUNK), 1)
    tril = (ri >= ci).astype(F32)
    triu = (ri <= ci).astype(F32)
    group = lax.broadcasted_iota(jnp.int32, (CHUNK, LANES), 1) // V_HEADS
    for c in range(tm // CHUNK):
        rows = slice(c * CHUNK, (c + 1) * CHUNK)
        gc = g[rows]
        pre = jnp.dot(tril, gc, preferred_element_type=F32, precision=lax.Precision.HIGHEST)
        suf = jnp.dot(triu, gc, preferred_element_type=F32, precision=lax.Precision.HIGHEST)
        o_ref[rows, :] = jnp.where(group == 0, pre, jnp.where(group == 2, suf, beta[rows]))


def _gdn_gates(ab, a_log, dt_bias):
    b, t, _ = ab.shape
    tm = _pick(t, 512, CHUNK)
    zeros = jnp.zeros((2, V_HEADS), F32)
    lane_vec = lambda p: jnp.pad(jnp.stack([p, zeros], axis=1).reshape(1, 4 * V_HEADS),
                                 ((0, 0), (0, LANES - 4 * V_HEADS)))
    return pl.pallas_call(
        functools.partial(_gdn_gate_kernel, tm=tm),
        grid=(b, t // tm),
        in_specs=[pl.BlockSpec((None, tm, LANES), lambda bi, i: (bi, i, 0)),
                  pl.BlockSpec((1, LANES), lambda bi, i: (0, 0)),
                  pl.BlockSpec((1, LANES), lambda bi, i: (0, 0))],
        out_specs=pl.BlockSpec((None, tm, LANES), lambda bi, i: (bi, i, 0)),
        out_shape=jax.ShapeDtypeStruct((b, t, LANES), F32),
        compiler_params=_cparams(("parallel", "parallel")),
        name="gdn_gates",
    )(ab, lane_vec(a_log), lane_vec(dt_bias))


def _halo_specs(tm, t, width, col_block, tile_of):
    per = tm // HALO
    last = t // HALO - 1

    def cur(*g):
        bi, i = tile_of(*g)
        return (bi, i, col_block)

    def prev(*g):
        bi, i = tile_of(*g)
        return (bi, jnp.maximum(i * per - 1, 0), col_block)

    def nxt(*g):
        bi, i = tile_of(*g)
        return (bi, jnp.minimum((i + 1) * per, last), col_block)

    return [pl.BlockSpec((None, tm, width), cur),
            pl.BlockSpec((None, HALO, width), prev),
            pl.BlockSpec((None, HALO, width), nxt)]


def _fill_ext(ext, cur_ref, prev_ref, next_ref, i, nt, tm):
    ext[0:HALO, :] = jnp.where(i > 0, prev_ref[...], 0.0)
    ext[HALO:HALO + tm, :] = cur_ref[...]
    ext[HALO + tm:, :] = jnp.where(i < nt - 1, next_ref[...], 0.0)


def _conv_taps(ext, w_ref, lanes, tm):
    acc = None
    for j in range(CONV_W):
        r0 = HALO - CONV_LEFT + j
        term = ext[r0:r0 + tm, lanes] * w_ref[j:j + 1, lanes]
        acc = term if acc is None else acc + term
    return acc


def _gdn_prep_kernel(cur_ref, prev_ref, next_ref, w_ref, q_ref, k_ref, v_ref, ext, *, tm):
    _fill_ext(ext, cur_ref, prev_ref, next_ref, pl.program_id(1), pl.num_programs(1), tm)
    qk_w = QK_HEADS * HEAD_DIM
    for h in range(2 * QK_HEADS + V_HEADS):
        lanes = slice(h * HEAD_DIM, (h + 1) * HEAD_DIM)
        y = _conv_taps(ext, w_ref, lanes, tm)
        y = y * _sigmoid(y)
        if h < 2 * QK_HEADS:
            y = y * lax.rsqrt(jnp.sum(y * y, axis=-1, keepdims=True) + EPS)
            if h < QK_HEADS:
                q_ref[:, lanes] = y * (HEAD_DIM ** -0.5)
            else:
                k_ref[:, h * HEAD_DIM - qk_w:(h + 1) * HEAD_DIM - qk_w] = y
        else:
            v_ref[:, h * HEAD_DIM - 2 * qk_w:(h + 1) * HEAD_DIM - 2 * qk_w] = y


def _gdn_prep(proj, conv_w):
    b, t, _ = proj.shape
    qk_w = QK_HEADS * HEAD_DIM
    v_w = V_HEADS * HEAD_DIM
    qkv_w = 2 * qk_w + v_w
    tm = _pick(t, 256, CHUNK)
    tile = lambda width: pl.BlockSpec((None, tm, width), lambda bi, i: (bi, i, 0))
    return pl.pallas_call(
        functools.partial(_gdn_prep_kernel, tm=tm),
        grid=(b, t // tm),
        in_specs=_halo_specs(tm, t, qkv_w, 0, lambda bi, i: (bi, i)) + [
            pl.BlockSpec((CONV_W, qkv_w), lambda bi, i: (0, 0))],
        out_specs=[tile(qk_w), tile(qk_w), tile(v_w)],
        out_shape=[jax.ShapeDtypeStruct((b, t, qk_w), F32),
                   jax.ShapeDtypeStruct((b, t, qk_w), F32),
                   jax.ShapeDtypeStruct((b, t, v_w), F32)],
        scratch_shapes=[pltpu.VMEM((tm + 2 * HALO, qkv_w), F32)],
        compiler_params=_cparams(("parallel", "parallel")),
        name="gdn_prep",
    )(proj, proj, proj, conv_w)


INV_BASE = 8


def _unit_triangular_inverse(lmat, eye, ri, ci):
    same_block = lambda size: (ri // size) == (ci // size)
    p = -jnp.where(same_block(INV_BASE), lmat, 0.0)
    p16 = p.astype(BF16)
    p2 = _dot(p16, p16)
    p2_16 = p2.astype(BF16)
    p4 = _dot(p2_16, p2_16)
    tinv = eye + p
    tinv = tinv + _dot(tinv.astype(BF16), p2_16)
    tinv = tinv + _dot(tinv.astype(BF16), p4.astype(BF16))
    size = INV_BASE
    while size < CHUNK:
        a_off = jnp.where(same_block(2 * size) & jnp.logical_not(same_block(size)), lmat, 0.0)
        t16 = tinv.astype(BF16)
        tinv = tinv - _dot(t16, _dot(a_off.astype(BF16), t16).astype(BF16))
        size *= 2
    return tinv


def _gdn_scan_kernel(qf_ref, kf_ref, vf_ref, cf_ref, rf_ref, qb_ref, kb_ref, vb_ref, cb_ref, rb_ref,
                     s0_ref, of_ref, ob_ref, sout_ref, s_scr, *, cpb):
    n = pl.program_id(2)

    @pl.when(n == 0)
    def _():
        s_scr[...] = s0_ref[...]

    ri = lax.broadcasted_iota(jnp.int32, (CHUNK, CHUNK), 0)
    ci = lax.broadcasted_iota(jnp.int32, (CHUNK, CHUNK), 1)
    eye = (ri == ci).astype(F32)
    incl = (ri >= ci, ri <= ci)
    strict = (ri > ci, ri < ci)
    refs = ((qf_ref, kf_ref, vf_ref, cf_ref, rf_ref, of_ref),
            (qb_ref, kb_ref, vb_ref, cb_ref, rb_ref, ob_ref))
    rep = V_HEADS // QK_HEADS

    def chunk_step(c, carry):
        for d in range(2):
            q_ref, k_ref, v_ref, c_ref, r_ref, o_ref = refs[d]
            cc = c if d == 0 else cpb - 1 - c
            off = pl.multiple_of(cc * CHUNK, CHUNK)
            q = q_ref[pl.ds(off, CHUNK), :]
            k = k_ref[pl.ds(off, CHUNK), :]
            cols = c_ref[pl.ds(off, CHUNK), :]
            rows = r_ref[cc]
            k16 = k.astype(BF16)
            kkt = _dot_nt(k16, k16)
            qkt = _dot_nt(q.astype(BF16), k16)
            for vh in range(rep):
                lane = d * 2 * rep + vh
                gcol = cols[:, lane:lane + 1]
                bcol = cols[:, lane + rep:lane + rep + 1]
                grow = rows[lane:lane + 1, :]
                gtot = gcol[CHUNK - 1:CHUNK, :] if d == 0 else gcol[0:1, :]
                diff = gcol - grow
                decay = jnp.where(incl[d], jnp.exp(jnp.where(incl[d], diff, 0.0)), 0.0)
                lmat = jnp.where(strict[d], kkt * bcol * decay, 0.0)
                qkm = qkt * decay
                tinv = _unit_triangular_inverse(lmat, eye, ri, ci)
                v = v_ref[pl.ds(off, CHUNK), vh * HEAD_DIM:(vh + 1) * HEAD_DIM]
                eg = jnp.exp(gcol)
                rhs = jnp.concatenate([v * bcol, k * (bcol * eg)], axis=1).astype(BF16)
                sol = _dot(tinv.astype(BF16), rhs)
                u = sol[:, :HEAD_DIM]
                w = sol[:, HEAD_DIM:]
                qd = q * eg
                kd = k * jnp.exp(gtot - gcol)
                si = d * rep + vh
                s = s_scr[si]
                ws = _dot(jnp.concatenate([w, qd], axis=0).astype(BF16), s.astype(BF16))
                vnew = u - ws[:CHUNK]
                vnew16 = vnew.astype(BF16)
                o = ws[CHUNK:] + _dot(qkm.astype(BF16), vnew16)
                s_scr[si] = s * jnp.exp(gtot) + _dot_tn(kd.astype(BF16), vnew16)
                o_ref[pl.ds(off, CHUNK), vh * HEAD_DIM:(vh + 1) * HEAD_DIM] = o
        return carry

    lax.fori_loop(0, cpb, chunk_step, 0)

    @pl.when(n == pl.num_programs(2) - 1)
    def _():
        sout_ref[...] = s_scr[...]


def _gdn_scan(qn, kn, v, gates, s0):
    b, t, _ = qn.shape
    rep = V_HEADS // QK_HEADS
    nch = 2 * 2 * rep
    g6 = gates[..., :4 * V_HEADS].reshape(b, t, 2, 2, QK_HEADS, rep)
    cols = g6.transpose(0, 4, 1, 2, 3, 5).reshape(b, QK_HEADS, t, nch)
    rows = cols.reshape(b, QK_HEADS, t // CHUNK, CHUNK, nch).transpose(0, 1, 2, 4, 3)
    blk = _pick(t, 4 * CHUNK, CHUNK)
    cpb = blk // CHUNK
    nb = t // blk
    fwd = lambda n: n
    bwd = lambda n: nb - 1 - n

    def specs(order):
        return [pl.BlockSpec((None, blk, HEAD_DIM), lambda bi, h, n: (bi, order(n), h)),
                pl.BlockSpec((None, blk, HEAD_DIM), lambda bi, h, n: (bi, order(n), h)),
                pl.BlockSpec((None, blk, rep * HEAD_DIM), lambda bi, h, n: (bi, order(n), h)),
                pl.BlockSpec((None, None, blk, nch), lambda bi, h, n: (bi, h, order(n), 0)),
                pl.BlockSpec((None, None, cpb, nch, CHUNK), lambda bi, h, n: (bi, h, order(n), 0, 0))]

    state_spec = pl.BlockSpec((None, None, 2 * rep, HEAD_DIM, HEAD_DIM), lambda bi, h, n: (bi, h, 0, 0, 0))
    o_shape = jax.ShapeDtypeStruct((b, t, V_HEADS * HEAD_DIM), F32)
    return pl.pallas_call(
        functools.partial(_gdn_scan_kernel, cpb=cpb),
        grid=(b, QK_HEADS, nb),
        in_specs=specs(fwd) + specs(bwd) + [state_spec],
        out_specs=[pl.BlockSpec((None, blk, rep * HEAD_DIM), lambda bi, h, n: (bi, fwd(n), h)),
                   pl.BlockSpec((None, blk, rep * HEAD_DIM), lambda bi, h, n: (bi, bwd(n), h)),
                   state_spec],
        out_shape=[o_shape, o_shape, jax.ShapeDtypeStruct(s0.shape, F32)],
        scratch_shapes=[pltpu.VMEM((2 * rep, HEAD_DIM, HEAD_DIM), F32)],
        compiler_params=_cparams(("parallel", "parallel", "arbitrary")),
        name="gdn_scan",
    )(qn, kn, v, cols, rows, qn, kn, v, cols, rows, s0)


def _lru_scan_kernel(fc_ref, fp_ref, fn_ref, bc_ref, bp_ref, bn_ref, cw_ref, cb_ref, wr_ref, br_ref,
                     wi_ref, bi_ref, lam_ref, h0_ref, hf_ref, hb_ref, hfin_ref, ext, carry, *, tm):
    i = pl.program_id(1)
    nt = pl.num_programs(1)

    @pl.when(i == 0)
    def _():
        carry[...] = h0_ref[...]

    width = cw_ref.shape[1]
    bs = width // LRU_BLOCKS
    row = lax.broadcasted_iota(jnp.int32, (tm, bs), 0)
    dirs = ((fc_ref, fp_ref, fn_ref, hf_ref, i), (bc_ref, bp_ref, bn_ref, hb_ref, nt - 1 - i))
    for d, (c_ref, p_ref, n_ref, o_ref, tile) in enumerate(dirs):
        _fill_ext(ext, c_ref, p_ref, n_ref, tile, nt, tm)
        for blk in range(LRU_BLOCKS):
            lanes = slice(blk * bs, (blk + 1) * bs)
            xc = _conv_taps(ext, cw_ref, lanes, tm) + cb_ref[:, lanes]
            xc16 = xc.astype(BF16)
            r = _sigmoid(_dot(xc16, wr_ref[d, blk]) + br_ref[d:d + 1, lanes])
            gi = _sigmoid(_dot(xc16, wi_ref[d, blk]) + bi_ref[d:d + 1, lanes])
            log_a = -RG_C * r * _softplus(-lam_ref[d:d + 1, lanes])
            a = jnp.exp(log_a)
            bv = jnp.sqrt(1.0 - jnp.exp(2.0 * log_a)) * (gi * xc)
            s = 1
            while s < tm:
                shift = s if d == 0 else tm - s
                keep = (row >= s) if d == 0 else (row < tm - s)
                a_prev = jnp.where(keep, pltpu.roll(a, shift, 0), 1.0)
                b_prev = jnp.where(keep, pltpu.roll(bv, shift, 0), 0.0)
                bv = a * b_prev + bv
                a = a * a_prev
                s *= 2
            h = a * carry[d, 0:1, lanes] + bv
            o_ref[:, lanes] = h
            carry[d, 0:1, lanes] = h[tm - 1:tm, :] if d == 0 else h[0:1, :]

    @pl.when(i == nt - 1)
    def _():
        hfin_ref[...] = carry[...]


def _lru_scan(proj, conv_w, conv_b, w_r, b_r, w_i, b_i, lam, h0):
    b, t, two_w = proj.shape
    width = two_w // 2
    tm = _pick(t, 256, CHUNK)
    nt = t // tm
    whole = lambda a: pl.BlockSpec(a.shape, lambda bi, i: (0,) * a.ndim)
    state_spec = pl.BlockSpec((None, 2, 1, width), lambda bi, i: (bi, 0, 0, 0))
    consts = [conv_w, conv_b.reshape(1, width), w_r, b_r, w_i, b_i, lam]
    h_shape = jax.ShapeDtypeStruct((b, t, width), F32)
    return pl.pallas_call(
        functools.partial(_lru_scan_kernel, tm=tm),
        grid=(b, nt),
        in_specs=(_halo_specs(tm, t, width, 0, lambda bi, i: (bi, i))
                  + _halo_specs(tm, t, width, 0, lambda bi, i: (bi, nt - 1 - i))
                  + [whole(a) for a in consts] + [state_spec]),
        out_specs=[pl.BlockSpec((None, tm, width), lambda bi, i: (bi, i, 0)),
                   pl.BlockSpec((None, tm, width), lambda bi, i: (bi, nt - 1 - i, 0)),
                   state_spec],
        out_shape=[h_shape, h_shape, jax.ShapeDtypeStruct(h0.shape, F32)],
        scratch_shapes=[pltpu.VMEM((tm + 2 * HALO, width), F32),
                        pltpu.VMEM((2, 1, width), F32)],
        compiler_params=_cparams(("parallel", "arbitrary")),
        name="lru_scan",
    )(proj, proj, proj, proj, proj, proj, *consts, h0)


def _residual_store(xo_ref, x_ref, y, gate, fg_ref, nw, d, rows):
    for k in range(nw):
        lanes = slice(k * d, (k + 1) * d) if nw > 1 else slice(None)
        yk = y[k * rows:(k + 1) * rows] if nw > 1 else y
        xn = x_ref[:, lanes] + gate * yk
        if fg_ref is not None:
            xn = xn * lax.rsqrt(jnp.mean(xn * xn, axis=-1, keepdims=True) + EPS) * fg_ref[...]
        xo_ref[:, lanes] = xn


def _gdn_out_kernel(*refs, nw, d, rows, final):
    of_ref, ob_ref, z_ref, ng_ref, w_ref, x_ref, gate_ref = refs[:7]
    fg_ref = refs[7] if final else None
    xo_ref = refs[-1]
    parts = []
    for h in range(V_HEADS):
        lanes = slice(h * HEAD_DIM, (h + 1) * HEAD_DIM)
        o = of_ref[:, lanes] + ob_ref[:, lanes]
        o = o * lax.rsqrt(jnp.mean(o * o, axis=-1, keepdims=True) + EPS) * ng_ref[...]
        z = z_ref[:, lanes]
        parts.append((o * (z * _sigmoid(z))).astype(BF16))
    y = _dot(jnp.concatenate(parts, axis=1), w_ref[...])
    _residual_store(xo_ref, x_ref, y, gate_ref[...], fg_ref, nw, d, rows)


def _lru_out_kernel(*refs, nw, d, rows, final):
    hf_ref, hb_ref, gt_ref, w_ref, x_ref, gate_ref = refs[:6]
    fg_ref = refs[6] if final else None
    xo_ref = refs[-1]
    gt = gt_ref[...]
    hg = ((hf_ref[...] + hb_ref[...]) * (gt * _sigmoid(gt))).astype(BF16)
    y = _dot(hg, w_ref[...])
    _residual_store(xo_ref, x_ref, y, gate_ref[...], fg_ref, nw, d, rows)


def _out_proj(kind, branch, gate_src, gate_col, extra, w_out, x, gate, final_g, *, col_major):
    b, t, d = x.shape
    wb = branch[0].shape[-1]
    tm, nw, view, xblock, xmap = _stream_tiling(t, d, col_major, 256)
    tile = lambda col: pl.BlockSpec((None, tm, wb), lambda bi, i: (bi, i, col))
    in_specs = [tile(0), tile(0), tile(gate_col)]
    args = [branch[0], branch[1], gate_src]
    if kind == "gdn":
        in_specs.append(pl.BlockSpec((1, HEAD_DIM), lambda bi, i: (0, 0)))
        args.append(extra.reshape(1, HEAD_DIM))
        body = _gdn_out_kernel
    else:
        body = _lru_out_kernel
    in_specs += [pl.BlockSpec(w_out.shape, lambda bi, i: (0, 0)),
                 pl.BlockSpec(xblock, lambda bi, i: xmap(bi, i)),
                 pl.BlockSpec((None, 1, d), lambda bi, i: (bi, 0, 0))]
    args += [w_out, view(x), gate]
    final = final_g is not None
    if final:
        in_specs.append(pl.BlockSpec((1, d), lambda bi, i: (0, 0)))
        args.append(final_g.reshape(1, d))
    xv = view(x)
    out = pl.pallas_call(
        functools.partial(body, nw=nw, d=d, rows=tm // nw, final=final),
        grid=(b, t // tm),
        in_specs=in_specs,
        out_specs=pl.BlockSpec(xblock, lambda bi, i: xmap(bi, i)),
        out_shape=jax.ShapeDtypeStruct(xv.shape, F32),
        compiler_params=_cparams(("parallel", "parallel")),
        name=kind + "_out",
    )(*args)
    return out.reshape(b, t, d)


def _gdn_layer(streams, mods, norm_g, w_in, conv_w, a_log, dt_bias, head_g, w_out, *, col_major, final_g,
               update_ctx):
    qkvz_w = 2 * QK_HEADS * HEAD_DIM + 2 * V_HEADS * HEAD_DIM
    w_main = w_in[:, :qkvz_w].astype(BF16)
    w_ab = jnp.pad(w_in[:, qkvz_w:], ((0, 0), (0, LANES - (w_in.shape[1] - qkvz_w)))).astype(BF16)
    w_out16 = w_out.astype(BF16)
    state = None
    outs = []
    for (x, cm, fg, need_y), (shift, scale, gate) in zip(
            ((streams[0], False, None, update_ctx), (streams[1], col_major, final_g, True)), mods):
        b = x.shape[0]
        if state is None:
            state = jnp.zeros((b, QK_HEADS, 2 * (V_HEADS // QK_HEADS), HEAD_DIM, HEAD_DIM), F32)
        proj, ab = _in_proj(x, norm_g, shift, scale, w_main, w_ab, col_major=cm)
        gates = _gdn_gates(ab, a_log, dt_bias)
        qn, kn, v = _gdn_prep(proj, conv_w)
        o_f, o_b, state = _gdn_scan(qn, kn, v, gates, state)
        if need_y:
            x = _out_proj("gdn", (o_f, o_b), proj, 2, head_g, w_out16, x, gate, fg, col_major=cm)
        outs.append(x)
    return outs


def _lru_layer(streams, mods, norm_g, w_in, conv_w, conv_b, w_r, b_r, w_i, b_i, lam, w_out, *, col_major,
               final_g, update_ctx):
    w_in16 = w_in.astype(BF16)
    w_r16 = w_r.astype(BF16)
    w_i16 = w_i.astype(BF16)
    w_out16 = w_out.astype(BF16)
    state = None
    outs = []
    for (x, cm, fg, need_y), (shift, scale, gate) in zip(
            ((streams[0], False, None, update_ctx), (streams[1], col_major, final_g, True)), mods):
        b = x.shape[0]
        if state is None:
            state = jnp.zeros((b, 2, 1, conv_w.shape[1]), F32)
        proj = _in_proj(x, norm_g, shift, scale, w_in16, None, col_major=cm)
        h_f, h_b, state = _lru_scan(proj, conv_w, conv_b, w_r16, b_r, w_i16, b_i, lam, state)
        if need_y:
            x = _out_proj("lru", (h_f, h_b), proj, 1, None, w_out16, x, gate, fg, col_major=cm)
        outs.append(x)
    return outs


def kernel(x, c, ctx, c_ctx, mod_w, mod_b, norm_g, gdn_w_in, gdn_conv, gdn_a_log, gdn_dt_bias, gdn_norm_g,
           gdn_w_out, lru_w_in, lru_conv_w, lru_conv_b, lru_w_r, lru_b_r, lru_w_i, lru_b_i, lru_lambda,
           lru_w_out, final_g):
    b, _, d = x.shape
    depth = mod_w.shape[0]
    assert b + 1 <= MOD_ROWS
    cond = jnp.concatenate([c, c_ctx[None], jnp.zeros((MOD_ROWS - b - 1, d), F32)], axis=0)
    mod = _modulation(cond, mod_w, mod_b)
    for i in range(depth):
        lat = [mod[i, :b, k * d:(k + 1) * d].reshape(b, 1, d) for k in range(3)]
        cmod = [jnp.broadcast_to(mod[i, b, k * d:(k + 1) * d].reshape(1, 1, d), (b, 1, d)) for k in range(3)]
        col_major = (i + i // 2) % 2 == 1
        last = i == depth - 1
        j = i // 2
        common = dict(col_major=col_major, final_g=final_g if last else None, update_ctx=not last)
        if i % 2 == 0:
            ctx, x = _gdn_layer((ctx, x), (cmod, lat), norm_g[i], gdn_w_in[j], gdn_conv[j], gdn_a_log[j],
                                gdn_dt_bias[j], gdn_norm_g[j], gdn_w_out[j], **common)
        else:
            ctx, x = _lru_layer((ctx, x), (cmod, lat), norm_g[i], lru_w_in[j], lru_conv_w[j], lru_conv_b[j],
                                lru_w_r[j], lru_b_r[j], lru_w_i[j], lru_b_i[j], lru_lambda[j], lru_w_out[j],
                                **common)
    return x
```

```python
import functools

import jax
import jax.numpy as jnp
from jax import lax
from jax.experimental import pallas as pl
from jax.experimental.pallas import tpu as pltpu

F32 = jnp.float32
BF16 = jnp.bfloat16

EPS = 1e-6
GRID_W = 64
HEAD_DIM = 128
QK_HEADS = 8
V_HEADS = 16
CHUNK = 64
CONV_W = 4
CONV_LEFT = 2
HALO = 8
LRU_BLOCKS = 4
RG_C = 8.0
LANES = 128
MOD_ROWS = 8

VMEM_LIMIT = 56 * 1024 * 1024


def _cparams(sem):
    return pltpu.CompilerParams(dimension_semantics=sem, vmem_limit_bytes=VMEM_LIMIT)


def _pick(total, target, mult):
    best = None
    for cand in range(mult, min(total, target) + 1, mult):
        if total % cand == 0:
            best = cand
    assert best is not None, (total, target, mult)
    return best


def _sigmoid(x):
    return 1.0 / (1.0 + jnp.exp(-x))


def _softplus(x):
    return jnp.maximum(x, 0.0) + jnp.log(1.0 + jnp.exp(-jnp.abs(x)))


def _dot(a, b):
    return jnp.dot(a, b, preferred_element_type=F32)


def _dot_nt(a, b):
    return lax.dot_general(a, b, (((1,), (1,)), ((), ())), preferred_element_type=F32)


def _dot_tn(a, b):
    return lax.dot_general(a, b, (((0,), (0,)), ((), ())), preferred_element_type=F32)


def _mod_kernel(c_ref, w_ref, b_ref, o_ref):
    c = c_ref[...]
    sc = c * _sigmoid(c)
    o_ref[...] = jnp.dot(sc, w_ref[...], preferred_element_type=F32,
                         precision=lax.Precision.HIGHEST) + b_ref[...]


def _modulation(cond, mod_w, mod_b):
    depth, d, n3 = mod_w.shape
    return pl.pallas_call(
        _mod_kernel,
        grid=(depth, n3 // d),
        in_specs=[pl.BlockSpec((MOD_ROWS, d), lambda i, j: (0, 0)),
                  pl.BlockSpec((None, d, d), lambda i, j: (i, 0, j)),
                  pl.BlockSpec((None, 1, d), lambda i, j: (i, 0, j))],
        out_specs=pl.BlockSpec((None, MOD_ROWS, d), lambda i, j: (i, 0, j)),
        out_shape=jax.ShapeDtypeStruct((depth, MOD_ROWS, n3), F32),
        compiler_params=_cparams(("parallel", "parallel")),
        name="modulation",
    )(cond, mod_w, mod_b.reshape(depth, 1, n3))


def _stream_tiling(t, d, col_major, target):
    if not col_major:
        tm = _pick(t, target, CHUNK)
        return tm, 1, (lambda a: a), (None, tm, d), (lambda b, i: (b, i, 0))
    rows = t // GRID_W
    assert rows * GRID_W == t and rows % 8 == 0
    nw = _pick(GRID_W, max(target // rows, 1), 1)
    tm = rows * nw
    return (tm, nw, (lambda a: a.reshape(a.shape[0], rows, GRID_W * d)),
            (None, rows, nw * d), (lambda b, i: (b, 0, i)))


def _load_stream_tile(x_ref, nw, d):
    if nw == 1:
        return x_ref[...]
    return jnp.concatenate([x_ref[:, k * d:(k + 1) * d] for k in range(nw)], axis=0)


def _in_proj_kernel(*refs, nw, d, has_extra):
    if has_extra:
        x_ref, g_ref, sh_ref, sc_ref, w_ref, we_ref, o_ref, oe_ref, h_scr = refs
    else:
        x_ref, g_ref, sh_ref, sc_ref, w_ref, o_ref, h_scr = refs

    @pl.when(pl.program_id(2) == 0)
    def _():
        x = _load_stream_tile(x_ref, nw, d)
        y = x * lax.rsqrt(jnp.mean(x * x, axis=-1, keepdims=True) + EPS) * g_ref[...]
        h = (y * (1.0 + sc_ref[...]) + sh_ref[...]).astype(BF16)
        h_scr[...] = h
        if has_extra:
            oe_ref[...] = _dot(h, we_ref[...])

    o_ref[...] = _dot(h_scr[...], w_ref[...])


def _in_proj(x, norm_g, shift, scale, w, w_extra, *, col_major):
    b, t, d = x.shape
    n = w.shape[1]
    tn = _pick(n, 1024, LANES)
    tm, nw, view, xblock, xmap = _stream_tiling(t, d, col_major, 1024)
    has_extra = w_extra is not None
    vec = pl.BlockSpec((None, 1, d), lambda bi, i, j: (bi, 0, 0))
    in_specs = [pl.BlockSpec(xblock, lambda bi, i, j: xmap(bi, i)),
                pl.BlockSpec((1, d), lambda bi, i, j: (0, 0)), vec, vec,
                pl.BlockSpec((d, tn), lambda bi, i, j: (0, j))]
    out_specs = [pl.BlockSpec((None, tm, tn), lambda bi, i, j: (bi, i, j))]
    out_shape = [jax.ShapeDtypeStruct((b, t, n), F32)]
    args = [view(x), norm_g.reshape(1, d), shift, scale, w]
    if has_extra:
        in_specs.append(pl.BlockSpec((d, LANES), lambda bi, i, j: (0, 0)))
        out_specs.append(pl.BlockSpec((None, tm, LANES), lambda bi, i, j: (bi, i, 0)))
        out_shape.append(jax.ShapeDtypeStruct((b, t, LANES), F32))
        args.append(w_extra)
    outs = pl.pallas_call(
        functools.partial(_in_proj_kernel, nw=nw, d=d, has_extra=has_extra),
        grid=(b, t // tm, n // tn),
        in_specs=in_specs, out_specs=out_specs, out_shape=out_shape,
        scratch_shapes=[pltpu.VMEM((tm, d), BF16)],
        compiler_params=_cparams(("parallel", "parallel", "arbitrary")),
        name="in_proj",
    )(*args)
    return outs if has_extra else outs[0]


def _gdn_gate_kernel(ab_ref, alog_ref, dtb_ref, o_ref, *, tm):
    x = ab_ref[...]
    g = -jnp.exp(alog_ref[...]) * _softplus(x + dtb_ref[...])
    beta = _sigmoid(x)
    ri = lax.broadcasted_iota(jnp.int32, (CHUNK, CHUNK), 0)
    ci = lax.broadcasted_iota(jnp.int32, (CHUNK, CHUNK), 1)
    tril = (ri >= ci).astype(F32)
    triu = (ri <= ci).astype(F32)
    group = lax.broadcasted_iota(jnp.int32, (CHUNK, LANES), 1) // V_HEADS
    for c in range(tm // CHUNK):
        rows = slice(c * CHUNK, (c + 1) * CHUNK)
        gc = g[rows]
        pre = jnp.dot(tril, gc, preferred_element_type=F32, precision=lax.Precision.HIGHEST)
        suf = jnp.dot(triu, gc, preferred_element_type=F32, precision=lax.Precision.HIGHEST)
        o_ref[rows, :] = jnp.where(group == 0, pre, jnp.where(group == 2, suf, beta[rows]))


def _gdn_gates(ab, a_log, dt_bias):
    b, t, _ = ab.shape
    tm = _pick(t, 512, CHUNK)
    zeros = jnp.zeros((2, V_HEADS), F32)
    lane_vec = lambda p: jnp.pad(jnp.stack([p, zeros], axis=1).reshape(1, 4 * V_HEADS),
                                 ((0, 0), (0, LANES - 4 * V_HEADS)))
    return pl.pallas_call(
        functools.partial(_gdn_gate_kernel, tm=tm),
        grid=(b, t // tm),
        in_specs=[pl.BlockSpec((None, tm, LANES), lambda bi, i: (bi, i, 0)),
                  pl.BlockSpec((1, LANES), lambda bi, i: (0, 0)),
                  pl.BlockSpec((1, LANES), lambda bi, i: (0, 0))],
        out_specs=pl.BlockSpec((None, tm, LANES), lambda bi, i: (bi, i, 0)),
        out_shape=jax.ShapeDtypeStruct((b, t, LANES), F32),
        compiler_params=_cparams(("parallel", "parallel")),
        name="gdn_gates",
    )(ab, lane_vec(a_log), lane_vec(dt_bias))


def _halo_specs(tm, t, width, col_block, tile_of):
    per = tm // HALO
    last = t // HALO - 1

    def cur(*g):
        bi, i = tile_of(*g)
        return (bi, i, col_block)

    def prev(*g):
        bi, i = tile_of(*g)
        return (bi, jnp.maximum(i * per - 1, 0), col_block)

    def nxt(*g):
        bi, i = tile_of(*g)
        return (bi, jnp.minimum((i + 1) * per, last), col_block)

    return [pl.BlockSpec((None, tm, width), cur),
            pl.BlockSpec((None, HALO, width), prev),
            pl.BlockSpec((None, HALO, width), nxt)]


def _fill_ext(ext, cur_ref, prev_ref, next_ref, i, nt, tm):
    ext[0:HALO, :] = jnp.where(i > 0, prev_ref[...], 0.0)
    ext[HALO:HALO + tm, :] = cur_ref[...]
    ext[HALO + tm:, :] = jnp.where(i < nt - 1, next_ref[...], 0.0)


def _conv_taps(ext, w_ref, lanes, tm):
    acc = None
    for j in range(CONV_W):
        r0 = HALO - CONV_LEFT + j
        term = ext[r0:r0 + tm, lanes] * w_ref[j:j + 1, lanes]
        acc = term if acc is None else acc + term
    return acc


def _gdn_prep_kernel(cur_ref, prev_ref, next_ref, w_ref, q_ref, k_ref, v_ref, ext, *, tm):
    _fill_ext(ext, cur_ref, prev_ref, next_ref, pl.program_id(1), pl.num_programs(1), tm)
    qk_w = QK_HEADS * HEAD_DIM
    for h in range(2 * QK_HEADS + V_HEADS):
        lanes = slice(h * HEAD_DIM, (h + 1) * HEAD_DIM)
        y = _conv_taps(ext, w_ref, lanes, tm)
        y = y * _sigmoid(y)
        if h < 2 * QK_HEADS:
            y = y * lax.rsqrt(jnp.sum(y * y, axis=-1, keepdims=True) + EPS)
            if h < QK_HEADS:
                q_ref[:, lanes] = y * (HEAD_DIM ** -0.5)
            else:
                k_ref[:, h * HEAD_DIM - qk_w:(h + 1) * HEAD_DIM - qk_w] = y
        else:
            v_ref[:, h * HEAD_DIM - 2 * qk_w:(h + 1) * HEAD_DIM - 2 * qk_w] = y


def _gdn_prep(proj, conv_w):
    b, t, _ = proj.shape
    qk_w = QK_HEADS * HEAD_DIM
    v_w = V_HEADS * HEAD_DIM
    qkv_w = 2 * qk_w + v_w
    tm = _pick(t, 256, CHUNK)
    tile = lambda width: pl.BlockSpec((None, tm, width), lambda bi, i: (bi, i, 0))
    return pl.pallas_call(
        functools.partial(_gdn_prep_kernel, tm=tm),
        grid=(b, t // tm),
        in_specs=_halo_specs(tm, t, qkv_w, 0, lambda bi, i: (bi, i)) + [
            pl.BlockSpec((CONV_W, qkv_w), lambda bi, i: (0, 0))],
        out_specs=[tile(qk_w), tile(qk_w), tile(v_w)],
        out_shape=[jax.ShapeDtypeStruct((b, t, qk_w), F32),
                   jax.ShapeDtypeStruct((b, t, qk_w), F32),
                   jax.ShapeDtypeStruct((b, t, v_w), F32)],
        scratch_shapes=[pltpu.VMEM((tm + 2 * HALO, qkv_w), F32)],
        compiler_params=_cparams(("parallel", "parallel")),
        name="gdn_prep",
    )(proj, proj, proj, conv_w)


INV_BASE = 8


def _unit_triangular_inverses(lmats, eye, ri, ci):
    same_block = lambda size: (ri // size) == (ci // size)
    idx = range(len(lmats))
    base = same_block(INV_BASE)
    p = [-jnp.where(base, l, 0.0) for l in lmats]
    p16 = [x.astype(BF16) for x in p]
    p2 = [_dot(x, x).astype(BF16) for x in p16]
    p4 = [_dot(x, x).astype(BF16) for x in p2]
    tinv = [eye + x for x in p]
    tinv = [tinv[i] + _dot(tinv[i].astype(BF16), p2[i]) for i in idx]
    tinv = [tinv[i] + _dot(tinv[i].astype(BF16), p4[i]) for i in idx]
    size = INV_BASE
    while size < CHUNK:
        off_diag = same_block(2 * size) & jnp.logical_not(same_block(size))
        a16 = [jnp.where(off_diag, l, 0.0).astype(BF16) for l in lmats]
        t16 = [x.astype(BF16) for x in tinv]
        at16 = [_dot(a16[i], t16[i]).astype(BF16) for i in idx]
        tinv = [tinv[i] - _dot(t16[i], at16[i]) for i in idx]
        size *= 2
    return tinv


def _gdn_scan_kernel(qf_ref, kf_ref, vf_ref, cf_ref, rf_ref, qb_ref, kb_ref, vb_ref, cb_ref, rb_ref,
                     s0_ref, of_ref, ob_ref, sout_ref, s_scr, *, cpb):
    n = pl.program_id(2)

    @pl.when(n == 0)
    def _():
        s_scr[...] = s0_ref[...]

    ri = lax.broadcasted_iota(jnp.int32, (CHUNK, CHUNK), 0)
    ci = lax.broadcasted_iota(jnp.int32, (CHUNK, CHUNK), 1)
    eye = (ri == ci).astype(F32)
    incl = (ri >= ci, ri <= ci)
    strict = (ri > ci, ri < ci)
    refs = ((qf_ref, kf_ref, vf_ref, cf_ref, rf_ref, of_ref),
            (qb_ref, kb_ref, vb_ref, cb_ref, rb_ref, ob_ref))
    rep = V_HEADS // QK_HEADS

    chains = [(d, vh) for d in range(2) for vh in range(rep)]
    nchain = len(chains)

    tiles = {}
    for c in range(cpb):
        for d in range(2):
            q_ref, k_ref, v_ref, c_ref, r_ref, _ = refs[d]
            cc = c if d == 0 else cpb - 1 - c
            sl = slice(cc * CHUNK, (cc + 1) * CHUNK)
            q = q_ref[sl, :]
            k = k_ref[sl, :]
            tiles[c, d] = dict(q=q, k=k, v=v_ref[sl, :], cols=c_ref[sl, :], rows=r_ref[cc], sl=sl,
                               q16=q.astype(BF16), k16=k.astype(BF16))
    kkt = {key: _dot_nt(x["k16"], x["k16"]) for key, x in tiles.items()}
    qkt = {key: _dot_nt(x["q16"], x["k16"]) for key, x in tiles.items()}
    lmat, qkm16, rhs16, qd, kd16, gl = [], [], [], [], [], []
    for c in range(cpb):
        for d, vh in chains:
            x = tiles[c, d]
            lane = d * 2 * rep + vh
            gcol = x["cols"][:, lane:lane + 1]
            bcol = x["cols"][:, lane + rep:lane + rep + 1]
            grow = x["rows"][lane:lane + 1, :]
            gtot = gcol[CHUNK - 1:CHUNK, :] if d == 0 else gcol[0:1, :]
            decay = jnp.where(incl[d], jnp.exp(jnp.where(incl[d], gcol - grow, 0.0)), 0.0)
            lmat.append(jnp.where(strict[d], kkt[c, d] * bcol * decay, 0.0))
            qkm16.append((qkt[c, d] * decay).astype(BF16))
            eg = jnp.exp(gcol)
            v = x["v"][:, vh * HEAD_DIM:(vh + 1) * HEAD_DIM]
            rhs16.append(jnp.concatenate([v * bcol, x["k"] * (bcol * eg)], axis=1).astype(BF16))
            qd.append(x["q"] * eg)
            kd16.append((x["k"] * jnp.exp(gtot - gcol)).astype(BF16))
            gl.append(jnp.exp(gtot))
    tinv = _unit_triangular_inverses(lmat, eye, ri, ci)
    sol = [_dot(tinv[j].astype(BF16), rhs16[j]) for j in range(len(lmat))]
    u = [x[:, :HEAD_DIM] for x in sol]
    wq16 = [jnp.concatenate([sol[j][:, HEAD_DIM:], qd[j]], axis=0).astype(BF16) for j in range(len(lmat))]

    s = [s_scr[i] for i in range(nchain)]
    for c in range(cpb):
        js = [c * nchain + i for i in range(nchain)]
        ws = [_dot(wq16[j], s[i].astype(BF16)) for i, j in enumerate(js)]
        vnew16 = [(u[j] - ws[i][:CHUNK]).astype(BF16) for i, j in enumerate(js)]
        o = [ws[i][CHUNK:] + _dot(qkm16[j], vnew16[i]) for i, j in enumerate(js)]
        s = [s[i] * gl[j] + _dot_tn(kd16[j], vnew16[i]) for i, j in enumerate(js)]
        for i, (d, vh) in enumerate(chains):
            refs[d][5][tiles[c, d]["sl"], vh * HEAD_DIM:(vh + 1) * HEAD_DIM] = o[i]
    for i in range(nchain):
        s_scr[i] = s[i]

    @pl.when(n == pl.num_programs(2) - 1)
    def _():
        sout_ref[...] = s_scr[...]


def _gdn_scan(qn, kn, v, gates, s0):
    b, t, _ = qn.shape
    rep = V_HEADS // QK_HEADS
    nch = 2 * 2 * rep
    g6 = gates[..., :4 * V_HEADS].reshape(b, t, 2, 2, QK_HEADS, rep)
    cols = g6.transpose(0, 4, 1, 2, 3, 5).reshape(b, QK_HEADS, t, nch)
    rows = cols.reshape(b, QK_HEADS, t // CHUNK, CHUNK, nch).transpose(0, 1, 2, 4, 3)
    blk = _pick(t, 4 * CHUNK, CHUNK)
    cpb = blk // CHUNK
    nb = t // blk
    fwd = lambda n: n
    bwd = lambda n: nb - 1 - n

    def specs(order):
        return [pl.BlockSpec((None, blk, HEAD_DIM), lambda bi, h, n: (bi, order(n), h)),
                pl.BlockSpec((None, blk, HEAD_DIM), lambda bi, h, n: (bi, order(n), h)),
                pl.BlockSpec((None, blk, rep * HEAD_DIM), lambda bi, h, n: (bi, order(n), h)),
                pl.BlockSpec((None, None, blk, nch), lambda bi, h, n: (bi, h, order(n), 0)),
                pl.BlockSpec((None, None, cpb, nch, CHUNK), lambda bi, h, n: (bi, h, order(n), 0, 0))]

    state_spec = pl.BlockSpec((None, None, 2 * rep, HEAD_DIM, HEAD_DIM), lambda bi, h, n: (bi, h, 0, 0, 0))
    o_shape = jax.ShapeDtypeStruct((b, t, V_HEADS * HEAD_DIM), F32)
    return pl.pallas_call(
        functools.partial(_gdn_scan_kernel, cpb=cpb),
        grid=(b, QK_HEADS, nb),
        in_specs=specs(fwd) + specs(bwd) + [state_spec],
        out_specs=[pl.BlockSpec((None, blk, rep * HEAD_DIM), lambda bi, h, n: (bi, fwd(n), h)),
                   pl.BlockSpec((None, blk, rep * HEAD_DIM), lambda bi, h, n: (bi, bwd(n), h)),
                   state_spec],
        out_shape=[o_shape, o_shape, jax.ShapeDtypeStruct(s0.shape, F32)],
        scratch_shapes=[pltpu.VMEM((2 * rep, HEAD_DIM, HEAD_DIM), F32)],
        compiler_params=_cparams(("parallel", "parallel", "arbitrary")),
        name="gdn_scan",
    )(qn, kn, v, cols, rows, qn, kn, v, cols, rows, s0)


def _lru_scan_kernel(fc_ref, fp_ref, fn_ref, bc_ref, bp_ref, bn_ref, cw_ref, cb_ref, wr_ref, br_ref,
                     wi_ref, bi_ref, lam_ref, h0_ref, hf_ref, hb_ref, hfin_ref, ext, carry, *, tm):
    i = pl.program_id(1)
    nt = pl.num_programs(1)

    @pl.when(i == 0)
    def _():
        carry[...] = h0_ref[...]

    width = cw_ref.shape[1]
    bs = width // LRU_BLOCKS
    row = lax.broadcasted_iota(jnp.int32, (tm, bs), 0)
    dirs = ((fc_ref, fp_ref, fn_ref, hf_ref, i), (bc_ref, bp_ref, bn_ref, hb_ref, nt - 1 - i))
    for d, (c_ref, p_ref, n_ref, o_ref, tile) in enumerate(dirs):
        _fill_ext(ext, c_ref, p_ref, n_ref, tile, nt, tm)
        for blk in range(LRU_BLOCKS):
            lanes = slice(blk * bs, (blk + 1) * bs)
            xc = _conv_taps(ext, cw_ref, lanes, tm) + cb_ref[:, lanes]
            xc16 = xc.astype(BF16)
            r = _sigmoid(_dot(xc16, wr_ref[d, blk]) + br_ref[d:d + 1, lanes])
            gi = _sigmoid(_dot(xc16, wi_ref[d, blk]) + bi_ref[d:d + 1, lanes])
            log_a = -RG_C * r * _softplus(-lam_ref[d:d + 1, lanes])
            a = jnp.exp(log_a)
            bv = jnp.sqrt(1.0 - jnp.exp(2.0 * log_a)) * (gi * xc)
            s = 1
            while s < tm:
                shift = s if d == 0 else tm - s
                keep = (row >= s) if d == 0 else (row < tm - s)
                a_prev = jnp.where(keep, pltpu.roll(a, shift, 0), 1.0)
                b_prev = jnp.where(keep, pltpu.roll(bv, shift, 0), 0.0)
                bv = a * b_prev + bv
                a = a * a_prev
                s *= 2
            h = a * carry[d, 0:1, lanes] + bv
            o_ref[:, lanes] = h
            carry[d, 0:1, lanes] = h[tm - 1:tm, :] if d == 0 else h[0:1, :]

    @pl.when(i == nt - 1)
    def _():
        hfin_ref[...] = carry[...]


def _lru_scan(proj, conv_w, conv_b, w_r, b_r, w_i, b_i, lam, h0):
    b, t, two_w = proj.shape
    width = two_w // 2
    tm = _pick(t, 256, CHUNK)
    nt = t // tm
    whole = lambda a: pl.BlockSpec(a.shape, lambda bi, i: (0,) * a.ndim)
    state_spec = pl.BlockSpec((None, 2, 1, width), lambda bi, i: (bi, 0, 0, 0))
    consts = [conv_w, conv_b.reshape(1, width), w_r, b_r, w_i, b_i, lam]
    h_shape = jax.ShapeDtypeStruct((b, t, width), F32)
    return pl.pallas_call(
        functools.partial(_lru_scan_kernel, tm=tm),
        grid=(b, nt),
        in_specs=(_halo_specs(tm, t, width, 0, lambda bi, i: (bi, i))
                  + _halo_specs(tm, t, width, 0, lambda bi, i: (bi, nt - 1 - i))
                  + [whole(a) for a in consts] + [state_spec]),
        out_specs=[pl.BlockSpec((None, tm, width), lambda bi, i: (bi, i, 0)),
                   pl.BlockSpec((None, tm, width), lambda bi, i: (bi, nt - 1 - i, 0)),
                   state_spec],
        out_shape=[h_shape, h_shape, jax.ShapeDtypeStruct(h0.shape, F32)],
        scratch_shapes=[pltpu.VMEM((tm + 2 * HALO, width), F32),
                        pltpu.VMEM((2, 1, width), F32)],
        compiler_params=_cparams(("parallel", "arbitrary")),
        name="lru_scan",
    )(proj, proj, proj, proj, proj, proj, *consts, h0)


def _residual_store(xo_ref, x_ref, y, gate, fg_ref, nw, d, rows):
    for k in range(nw):
        lanes = slice(k * d, (k + 1) * d) if nw > 1 else slice(None)
        yk = y[k * rows:(k + 1) * rows] if nw > 1 else y
        xn = x_ref[:, lanes] + gate * yk
        if fg_ref is not None:
            xn = xn * lax.rsqrt(jnp.mean(xn * xn, axis=-1, keepdims=True) + EPS) * fg_ref[...]
        xo_ref[:, lanes] = xn


def _gdn_out_kernel(*refs, nw, d, rows, final):
    of_ref, ob_ref, z_ref, ng_ref, w_ref, x_ref, gate_ref = refs[:7]
    fg_ref = refs[7] if final else None
    xo_ref = refs[-1]
    parts = []
    for h in range(V_HEADS):
        lanes = slice(h * HEAD_DIM, (h + 1) * HEAD_DIM)
        o = of_ref[:, lanes] + ob_ref[:, lanes]
        o = o * lax.rsqrt(jnp.mean(o * o, axis=-1, keepdims=True) + EPS) * ng_ref[...]
        z = z_ref[:, lanes]
        parts.append((o * (z * _sigmoid(z))).astype(BF16))
    y = _dot(jnp.concatenate(parts, axis=1), w_ref[...])
    _residual_store(xo_ref, x_ref, y, gate_ref[...], fg_ref, nw, d, rows)


def _lru_out_kernel(*refs, nw, d, rows, final):
    hf_ref, hb_ref, gt_ref, w_ref, x_ref, gate_ref = refs[:6]
    fg_ref = refs[6] if final else None
    xo_ref = refs[-1]
    gt = gt_ref[...]
    hg = ((hf_ref[...] + hb_ref[...]) * (gt * _sigmoid(gt))).astype(BF16)
    y = _dot(hg, w_ref[...])
    _residual_store(xo_ref, x_ref, y, gate_ref[...], fg_ref, nw, d, rows)


def _out_proj(kind, branch, gate_src, gate_col, extra, w_out, x, gate, final_g, *, col_major):
    b, t, d = x.shape
    wb = branch[0].shape[-1]
    tm, nw, view, xblock, xmap = _stream_tiling(t, d, col_major, 256)
    tile = lambda col: pl.BlockSpec((None, tm, wb), lambda bi, i: (bi, i, col))
    in_specs = [tile(0), tile(0), tile(gate_col)]
    args = [branch[0], branch[1], gate_src]
    if kind == "gdn":
        in_specs.append(pl.BlockSpec((1, HEAD_DIM), lambda bi, i: (0, 0)))
        args.append(extra.reshape(1, HEAD_DIM))
        body = _gdn_out_kernel
    else:
        body = _lru_out_kernel
    in_specs += [pl.BlockSpec(w_out.shape, lambda bi, i: (0, 0)),
                 pl.BlockSpec(xblock, lambda bi, i: xmap(bi, i)),
                 pl.BlockSpec((None, 1, d), lambda bi, i: (bi, 0, 0))]
    args += [w_out, view(x), gate]
    final = final_g is not None
    if final:
        in_specs.append(pl.BlockSpec((1, d), lambda bi, i: (0, 0)))
        args.append(final_g.reshape(1, d))
    xv = view(x)
    out = pl.pallas_call(
        functools.partial(body, nw=nw, d=d, rows=tm // nw, final=final),
        grid=(b, t // tm),
        in_specs=in_specs,
        out_specs=pl.BlockSpec(xblock, lambda bi, i: xmap(bi, i)),
        out_shape=jax.ShapeDtypeStruct(xv.shape, F32),
        compiler_params=_cparams(("parallel", "parallel")),
        name=kind + "_out",
    )(*args)
    return out.reshape(b, t, d)


def _gdn_layer(streams, mods, norm_g, w_in, conv_w, a_log, dt_bias, head_g, w_out, *, col_major, final_g,
               update_ctx):
    qkvz_w = 2 * QK_HEADS * HEAD_DIM + 2 * V_HEADS * HEAD_DIM
    w_main = w_in[:, :qkvz_w].astype(BF16)
    w_ab = jnp.pad(w_in[:, qkvz_w:], ((0, 0), (0, LANES - (w_in.shape[1] - qkvz_w)))).astype(BF16)
    w_out16 = w_out.astype(BF16)
    state = None
    outs = []
    for (x, cm, fg, need_y), (shift, scale, gate) in zip(
            ((streams[0], False, None, update_ctx), (streams[1], col_major, final_g, True)), mods):
        b = x.shape[0]
        if state is None:
            state = jnp.zeros((b, QK_HEADS, 2 * (V_HEADS // QK_HEADS), HEAD_DIM, HEAD_DIM), F32)
        proj, ab = _in_proj(x, norm_g, shift, scale, w_main, w_ab, col_major=cm)
        gates = _gdn_gates(ab, a_log, dt_bias)
        qn, kn, v = _gdn_prep(proj, conv_w)
        o_f, o_b, state = _gdn_scan(qn, kn, v, gates, state)
        if need_y:
            x = _out_proj("gdn", (o_f, o_b), proj, 2, head_g, w_out16, x, gate, fg, col_major=cm)
        outs.append(x)
    return outs


def _lru_layer(streams, mods, norm_g, w_in, conv_w, conv_b, w_r, b_r, w_i, b_i, lam, w_out, *, col_major,
               final_g, update_ctx):
    w_in16 = w_in.astype(BF16)
    w_r16 = w_r.astype(BF16)
    w_i16 = w_i.astype(BF16)
    w_out16 = w_out.astype(BF16)
    state = None
    outs = []
    for (x, cm, fg, need_y), (shift, scale, gate) in zip(
            ((streams[0], False, None, update_ctx), (streams[1], col_major, final_g, True)), mods):
        b = x.shape[0]
        if state is None:
            state = jnp.zeros((b, 2, 1, conv_w.shape[1]), F32)
        proj = _in_proj(x, norm_g, shift, scale, w_in16, None, col_major=cm)
        h_f, h_b, state = _lru_scan(proj, conv_w, conv_b, w_r16, b_r, w_i16, b_i, lam, state)
        if need_y:
            x = _out_proj("lru", (h_f, h_b), proj, 1, None, w_out16, x, gate, fg, col_major=cm)
        outs.append(x)
    return outs


def kernel(x, c, ctx, c_ctx, mod_w, mod_b, norm_g, gdn_w_in, gdn_conv, gdn_a_log, gdn_dt_bias, gdn_norm_g,
           gdn_w_out, lru_w_in, lru_conv_w, lru_conv_b, lru_w_r, lru_b_r, lru_w_i, lru_b_i, lru_lambda,
           lru_w_out, final_g):
    b, _, d = x.shape
    depth = mod_w.shape[0]
    assert b + 1 <= MOD_ROWS
    cond = jnp.concatenate([c, c_ctx[None], jnp.zeros((MOD_ROWS - b - 1, d), F32)], axis=0)
    mod = _modulation(cond, mod_w, mod_b)
    for i in range(depth):
        lat = [mod[i, :b, k * d:(k + 1) * d].reshape(b, 1, d) for k in range(3)]
        cmod = [jnp.broadcast_to(mod[i, b, k * d:(k + 1) * d].reshape(1, 1, d), (b, 1, d)) for k in range(3)]
        col_major = (i + i // 2) % 2 == 1
        last = i == depth - 1
        j = i // 2
        common = dict(col_major=col_major, final_g=final_g if last else None, update_ctx=not last)
        if i % 2 == 0:
            ctx, x = _gdn_layer((ctx, x), (cmod, lat), norm_g[i], gdn_w_in[j], gdn_conv[j], gdn_a_log[j],
                                gdn_dt_bias[j], gdn_norm_g[j], gdn_w_out[j], **common)
        else:
            ctx, x = _lru_layer((ctx, x), (cmod, lat), norm_g[i], lru_w_in[j], lru_conv_w[j], lru_conv_b[j],
                                lru_w_r[j], lru_b_r[j], lru_w_i[j], lru_b_i[j], lru_lambda[j], lru_w_out[j],
                                **common)
    return x
```

```python
import functools

import jax
import jax.numpy as jnp
from jax import lax
from jax.experimental import pallas as pl
from jax.experimental.pallas import tpu as pltpu

F32 = jnp.float32
BF16 = jnp.bfloat16

EPS = 1e-6
GRID_W = 64
HEAD_DIM = 128
QK_HEADS = 8
V_HEADS = 16
CHUNK = 64
CONV_W = 4
CONV_LEFT = 2
HALO = 8
LRU_BLOCKS = 4
RG_C = 8.0
LANES = 128
MOD_ROWS = 8

VMEM_LIMIT = 56 * 1024 * 1024


def _cparams(sem):
    return pltpu.CompilerParams(dimension_semantics=sem, vmem_limit_bytes=VMEM_LIMIT)


def _pick(total, target, mult):
    best = None
    for cand in range(mult, min(total, target) + 1, mult):
        if total % cand == 0:
            best = cand
    assert best is not None, (total, target, mult)
    return best


def _sigmoid(x):
    return 1.0 / (1.0 + jnp.exp(-x))


def _softplus(x):
    return jnp.maximum(x, 0.0) + jnp.log(1.0 + jnp.exp(-jnp.abs(x)))


def _dot(a, b):
    return jnp.dot(a, b, preferred_element_type=F32)


def _dot_nt(a, b):
    return lax.dot_general(a, b, (((1,), (1,)), ((), ())), preferred_element_type=F32)


def _dot_tn(a, b):
    return lax.dot_general(a, b, (((0,), (0,)), ((), ())), preferred_element_type=F32)


def _mod_kernel(c_ref, w_ref, b_ref, o_ref):
    c = c_ref[...]
    sc = c * _sigmoid(c)
    o_ref[...] = jnp.dot(sc, w_ref[...], preferred_element_type=F32,
                         precision=lax.Precision.HIGHEST) + b_ref[...]


def _modulation(cond, mod_w, mod_b):
    depth, d, n3 = mod_w.shape
    return pl.pallas_call(
        _mod_kernel,
        grid=(depth, n3 // d),
        in_specs=[pl.BlockSpec((MOD_ROWS, d), lambda i, j: (0, 0)),
                  pl.BlockSpec((None, d, d), lambda i, j: (i, 0, j)),
                  pl.BlockSpec((None, 1, d), lambda i, j: (i, 0, j))],
        out_specs=pl.BlockSpec((None, MOD_ROWS, d), lambda i, j: (i, 0, j)),
        out_shape=jax.ShapeDtypeStruct((depth, MOD_ROWS, n3), F32),
        compiler_params=_cparams(("parallel", "parallel")),
        name="modulation",
    )(cond, mod_w, mod_b.reshape(depth, 1, n3))


def _stream_tiling(t, d, col_major, target):
    if not col_major:
        tm = _pick(t, target, CHUNK)
        return tm, 1, (lambda a: a), (None, tm, d), (lambda b, i: (b, i, 0))
    rows = t // GRID_W
    assert rows * GRID_W == t and rows % 8 == 0
    nw = _pick(GRID_W, max(target // rows, 1), 1)
    tm = rows * nw
    return (tm, nw, (lambda a: a.reshape(a.shape[0], rows, GRID_W * d)),
            (None, rows, nw * d), (lambda b, i: (b, 0, i)))


def _load_stream_tile(x_ref, nw, d):
    if nw == 1:
        return x_ref[...]
    return jnp.concatenate([x_ref[:, k * d:(k + 1) * d] for k in range(nw)], axis=0)


def _in_proj_kernel(*refs, nw, d, has_extra):
    if has_extra:
        x_ref, g_ref, sh_ref, sc_ref, w_ref, we_ref, o_ref, oe_ref, h_scr = refs
    else:
        x_ref, g_ref, sh_ref, sc_ref, w_ref, o_ref, h_scr = refs

    @pl.when(pl.program_id(2) == 0)
    def _():
        x = _load_stream_tile(x_ref, nw, d)
        y = x * lax.rsqrt(jnp.mean(x * x, axis=-1, keepdims=True) + EPS) * g_ref[...]
        h = (y * (1.0 + sc_ref[...]) + sh_ref[...]).astype(BF16)
        h_scr[...] = h
        if has_extra:
            oe_ref[...] = _dot(h, we_ref[...])

    o_ref[...] = _dot(h_scr[...], w_ref[...])


def _in_proj(x, norm_g, shift, scale, w, w_extra, *, col_major):
    b, t, d = x.shape
    n = w.shape[1]
    tn = _pick(n, 1024, LANES)
    tm, nw, view, xblock, xmap = _stream_tiling(t, d, col_major, 1024)
    has_extra = w_extra is not None
    vec = pl.BlockSpec((None, 1, d), lambda bi, i, j: (bi, 0, 0))
    in_specs = [pl.BlockSpec(xblock, lambda bi, i, j: xmap(bi, i)),
                pl.BlockSpec((1, d), lambda bi, i, j: (0, 0)), vec, vec,
                pl.BlockSpec((d, tn), lambda bi, i, j: (0, j))]
    out_specs = [pl.BlockSpec((None, tm, tn), lambda bi, i, j: (bi, i, j))]
    out_shape = [jax.ShapeDtypeStruct((b, t, n), F32)]
    args = [view(x), norm_g.reshape(1, d), shift, scale, w]
    if has_extra:
        in_specs.append(pl.BlockSpec((d, LANES), lambda bi, i, j: (0, 0)))
        out_specs.append(pl.BlockSpec((None, tm, LANES), lambda bi, i, j: (bi, i, 0)))
        out_shape.append(jax.ShapeDtypeStruct((b, t, LANES), F32))
        args.append(w_extra)
    outs = pl.pallas_call(
        functools.partial(_in_proj_kernel, nw=nw, d=d, has_extra=has_extra),
        grid=(b, t // tm, n // tn),
        in_specs=in_specs, out_specs=out_specs, out_shape=out_shape,
        scratch_shapes=[pltpu.VMEM((tm, d), BF16)],
        compiler_params=_cparams(("parallel", "parallel", "arbitrary")),
        name="in_proj",
    )(*args)
    return outs if has_extra else outs[0]


def _gdn_gate_kernel(ab_ref, alog_ref, dtb_ref, o_ref, *, tm):
    x = ab_ref[...]
    g = -jnp.exp(alog_ref[...]) * _softplus(x + dtb_ref[...])
    beta = _sigmoid(x)
    ri = lax.broadcasted_iota(jnp.int32, (CHUNK, CHUNK), 0)
    ci = lax.broadcasted_iota(jnp.int32, (CHUNK, CHUNK), 1)
    tril = (ri >= ci).astype(F32)
    triu = (ri <= ci).astype(F32)
    group = lax.broadcasted_iota(jnp.int32, (CHUNK, LANES), 1) // V_HEADS
    for c in range(tm // CHUNK):
        rows = slice(c * CHUNK, (c + 1) * CHUNK)
        gc = g[rows]
        pre = jnp.dot(tril, gc, preferred_element_type=F32, precision=lax.Precision.HIGHEST)
        suf = jnp.dot(triu, gc, preferred_element_type=F32, precision=lax.Precision.HIGHEST)
        o_ref[rows, :] = jnp.where(group == 0, pre, jnp.where(group == 2, suf, beta[rows]))


def _gdn_gates(ab, a_log, dt_bias):
    b, t, _ = ab.shape
    tm = _pick(t, 512, CHUNK)
    zeros = jnp.zeros((2, V_HEADS), F32)
    lane_vec = lambda p: jnp.pad(jnp.stack([p, zeros], axis=1).reshape(1, 4 * V_HEADS),
                                 ((0, 0), (0, LANES - 4 * V_HEADS)))
    return pl.pallas_call(
        functools.partial(_gdn_gate_kernel, tm=tm),
        grid=(b, t // tm),
        in_specs=[pl.BlockSpec((None, tm, LANES), lambda bi, i: (bi, i, 0)),
                  pl.BlockSpec((1, LANES), lambda bi, i: (0, 0)),
                  pl.BlockSpec((1, LANES), lambda bi, i: (0, 0))],
        out_specs=pl.BlockSpec((None, tm, LANES), lambda bi, i: (bi, i, 0)),
        out_shape=jax.ShapeDtypeStruct((b, t, LANES), F32),
        compiler_params=_cparams(("parallel", "parallel")),
        name="gdn_gates",
    )(ab, lane_vec(a_log), lane_vec(dt_bias))


def _halo_specs(tm, t, width, col_block, tile_of):
    per = tm // HALO
    last = t // HALO - 1

    def cur(*g):
        bi, i = tile_of(*g)
        return (bi, i, col_block)

    def prev(*g):
        bi, i = tile_of(*g)
        return (bi, jnp.maximum(i * per - 1, 0), col_block)

    def nxt(*g):
        bi, i = tile_of(*g)
        return (bi, jnp.minimum((i + 1) * per, last), col_block)

    return [pl.BlockSpec((None, tm, width), cur),
            pl.BlockSpec((None, HALO, width), prev),
            pl.BlockSpec((None, HALO, width), nxt)]


def _fill_ext(ext, cur_ref, prev_ref, next_ref, i, nt, tm):
    ext[0:HALO, :] = jnp.where(i > 0, prev_ref[...], 0.0)
    ext[HALO:HALO + tm, :] = cur_ref[...]
    ext[HALO + tm:, :] = jnp.where(i < nt - 1, next_ref[...], 0.0)


def _conv_taps(ext, w_ref, lanes, tm):
    acc = None
    for j in range(CONV_W):
        r0 = HALO - CONV_LEFT + j
        term = ext[r0:r0 + tm, lanes] * w_ref[j:j + 1, lanes]
        acc = term if acc is None else acc + term
    return acc


def _gdn_prep_kernel(cur_ref, prev_ref, next_ref, w_ref, q_ref, k_ref, v_ref, kt_ref, ext, *, tm):
    _fill_ext(ext, cur_ref, prev_ref, next_ref, pl.program_id(1), pl.num_programs(1), tm)
    qk_w = QK_HEADS * HEAD_DIM
    for h in range(2 * QK_HEADS + V_HEADS):
        lanes = slice(h * HEAD_DIM, (h + 1) * HEAD_DIM)
        y = _conv_taps(ext, w_ref, lanes, tm)
        y = y * _sigmoid(y)
        if h < 2 * QK_HEADS:
            y = y * lax.rsqrt(jnp.sum(y * y, axis=-1, keepdims=True) + EPS)
            if h < QK_HEADS:
                q_ref[:, lanes] = (y * (HEAD_DIM ** -0.5)).astype(BF16)
            else:
                kh = h - QK_HEADS
                k_ref[:, kh * HEAD_DIM:(kh + 1) * HEAD_DIM] = y.astype(BF16)
                for c in range(tm // CHUNK):
                    yc = y[c * CHUNK:(c + 1) * CHUNK]
                    kt_ref[kh, c] = jnp.concatenate([yc, yc], axis=0).T.astype(BF16)
        else:
            v_ref[:, h * HEAD_DIM - 2 * qk_w:(h + 1) * HEAD_DIM - 2 * qk_w] = y.astype(BF16)


def _gdn_prep(proj, conv_w):
    b, t, _ = proj.shape
    qk_w = QK_HEADS * HEAD_DIM
    v_w = V_HEADS * HEAD_DIM
    qkv_w = 2 * qk_w + v_w
    tm = _pick(t, 256, CHUNK)
    tile = lambda width: pl.BlockSpec((None, tm, width), lambda bi, i: (bi, i, 0))
    kt_block = (None, QK_HEADS, tm // CHUNK, HEAD_DIM, 2 * CHUNK)
    return pl.pallas_call(
        functools.partial(_gdn_prep_kernel, tm=tm),
        grid=(b, t // tm),
        in_specs=_halo_specs(tm, t, qkv_w, 0, lambda bi, i: (bi, i)) + [
            pl.BlockSpec((CONV_W, qkv_w), lambda bi, i: (0, 0))],
        out_specs=[tile(qk_w), tile(qk_w), tile(v_w),
                   pl.BlockSpec(kt_block, lambda bi, i: (bi, 0, i, 0, 0))],
        out_shape=[jax.ShapeDtypeStruct((b, t, qk_w), BF16),
                   jax.ShapeDtypeStruct((b, t, qk_w), BF16),
                   jax.ShapeDtypeStruct((b, t, v_w), BF16),
                   jax.ShapeDtypeStruct((b, QK_HEADS, t // CHUNK, HEAD_DIM, 2 * CHUNK), BF16)],
        scratch_shapes=[pltpu.VMEM((tm + 2 * HALO, qkv_w), F32)],
        compiler_params=_cparams(("parallel", "parallel")),
        name="gdn_prep",
    )(proj, proj, proj, conv_w)


INV_BASE = 8


def _block_diag2(pair, first):
    return jnp.concatenate([jnp.where(first, pair, 0.0), jnp.where(first, 0.0, pair)], axis=0).astype(BF16)


SCAN_STAGES = 13
SCAN_SKEW = 2
SCAN_CHUNKS = 16


def _gdn_scan_kernel(qf_ref, kf_ref, tf_ref, vf_ref, cf_ref, rf_ref, qb_ref, kb_ref, tb_ref, vb_ref, cb_ref,
                     rb_ref, s0_ref, of_ref, ob_ref, sout_ref, s_scr, *, cpb, hpb):
    n = pl.program_id(2)

    @pl.when(n == 0)
    def _():
        s_scr[...] = s0_ref[...]

    ri = lax.broadcasted_iota(jnp.int32, (CHUNK, 2 * CHUNK), 0)
    lane = lax.broadcasted_iota(jnp.int32, (CHUNK, 2 * CHUNK), 1)
    first = lane < CHUNK
    first_row = first[0:1]
    ci = jnp.where(first, lane, lane - CHUNK)
    eye = (ri == ci).astype(F32)
    incl = (ri >= ci, ri <= ci)
    strict = (ri > ci, ri < ci)
    same_block = lambda size: (ri // size) == (ci // size)
    refs = ((qf_ref, kf_ref, tf_ref, vf_ref, cf_ref, rf_ref, of_ref),
            (qb_ref, kb_ref, tb_ref, vb_ref, cb_ref, rb_ref, ob_ref))
    pairs = [(hl, d) for hl in range(hpb) for d in range(2)]
    hd = HEAD_DIM
    bd = lambda xs: [_block_diag2(x, first) for x in xs]
    zero_s = jnp.zeros((hd, hd), BF16)
    zero_v = jnp.zeros((CHUNK, hd), BF16)
    state = [s_scr[hl, d] for hl, d in pairs]

    def chunk_stages(c):
        probs = []
        for hl, d in pairs:
            cc = c if d == 0 else cpb - 1 - c
            probs.append((hl, d, cc, slice(cc * CHUNK, (cc + 1) * CHUNK)))
        idx = range(len(probs))
        gram = [_dot(jnp.concatenate([refs[d][0][sl, hl * hd:(hl + 1) * hd],
                                      refs[d][1][sl, hl * hd:(hl + 1) * hd]], axis=0), refs[d][2][hl, cc])
                for hl, d, cc, sl in probs]
        yield
        lmat, lhs2, qd, scale_u, scale_w, gl = [], [], [], [], [], []
        for j, (hl, d, cc, sl) in enumerate(probs):
            q_ref, k_ref, t_ref, v_ref, c_ref, r_ref, _ = refs[d]
            cols = c_ref[hl, sl, :]
            rows = r_ref[hl, cc]
            grow = rows[2 * d:2 * d + 1, :]
            brow = rows[2 * d + 1:2 * d + 2, :]
            gcols = [cols[:, d * 4 + vh:d * 4 + vh + 1] for vh in range(2)]
            bcols = [cols[:, d * 4 + 2 + vh:d * 4 + 3 + vh] for vh in range(2)]
            gtots = [g[CHUNK - 1:CHUNK, :] if d == 0 else g[0:1, :] for g in gcols]
            gpair = jnp.where(first, gcols[0], gcols[1])
            bpair = jnp.where(first, bcols[0], bcols[1])
            decay = jnp.where(incl[d], jnp.exp(jnp.where(incl[d], gpair - grow, 0.0)), 0.0)
            lmat.append(jnp.where(strict[d], gram[j][CHUNK:] * bpair * decay, 0.0))
            qkm = gram[j][:CHUNK] * decay
            kdt = t_ref[hl, cc].astype(F32) * jnp.exp(jnp.where(first_row, gtots[0], gtots[1]) - grow)
            lhs2.append(jnp.concatenate([qkm, kdt], axis=0).astype(BF16))
            q = q_ref[sl, hl * hd:(hl + 1) * hd].astype(F32)
            qd.append(jnp.concatenate([q * jnp.exp(g) for g in gcols], axis=1))
            scale_u.append(brow)
            scale_w.append(brow * jnp.exp(grow))
            gl.append(jnp.concatenate([jnp.broadcast_to(jnp.exp(g), (1, hd)) for g in gtots], axis=1))
        base = same_block(INV_BASE)
        p = [-jnp.where(base, l, 0.0) for l in lmat]
        p2 = [_dot(p[i].astype(BF16), m) for i, m in enumerate(bd(p))]
        yield
        p2_bd = bd(p2)
        p4 = [_dot(p2[i].astype(BF16), p2_bd[i]) for i in idx]
        tinv = [eye + x for x in p]
        tinv = [tinv[i] + _dot(tinv[i].astype(BF16), p2_bd[i]) for i in idx]
        yield
        tinv = [tinv[i] + _dot(tinv[i].astype(BF16), m) for i, m in enumerate(bd(p4))]
        yield
        size = INV_BASE
        while size < CHUNK:
            off_diag = same_block(2 * size) & jnp.logical_not(same_block(size))
            a16 = [jnp.where(off_diag, l, 0.0).astype(BF16) for l in lmat]
            at = [_dot(a16[i], m) for i, m in enumerate(bd(tinv))]
            yield
            tinv = [tinv[i] - _dot(tinv[i].astype(BF16), m) for i, m in enumerate(bd(at))]
            yield
            size *= 2
        u, w = [], []
        for j, (hl, d, cc, sl) in enumerate(probs):
            k16 = refs[d][1][sl, hl * hd:(hl + 1) * hd]
            v16 = [refs[d][3][sl, (hl * 2 + vh) * hd:(hl * 2 + vh + 1) * hd] for vh in range(2)]
            v_bd16 = jnp.concatenate([jnp.concatenate([v16[0], zero_v], axis=1),
                                      jnp.concatenate([zero_v, v16[1]], axis=1)], axis=0)
            k_bd16 = jnp.concatenate([jnp.concatenate([k16, zero_v], axis=1),
                                      jnp.concatenate([zero_v, k16], axis=1)], axis=0)
            u.append(_dot((tinv[j] * scale_u[j]).astype(BF16), v_bd16))
            w.append(_dot((tinv[j] * scale_w[j]).astype(BF16), k_bd16))
        yield
        wq16 = [jnp.concatenate([w[j], qd[j]], axis=0).astype(BF16) for j in idx]
        s16 = [x.astype(BF16) for x in state]
        s_bd = [jnp.concatenate([jnp.concatenate([x[:, :hd], zero_s], axis=1),
                                 jnp.concatenate([zero_s, x[:, hd:]], axis=1)], axis=0) for x in s16]
        ws = [_dot(wq16[i], s_bd[i]) for i in idx]
        yield
        vnew16 = [(u[i] - ws[i][:CHUNK]).astype(BF16) for i in idx]
        v_bd = [jnp.concatenate([jnp.concatenate([x[:, :hd], zero_v], axis=1),
                                 jnp.concatenate([zero_v, x[:, hd:]], axis=1)], axis=0) for x in vnew16]
        upd = [_dot(lhs2[i], v_bd[i]) for i in idx]
        for i, (hl, d, cc, sl) in enumerate(probs):
            state[i] = state[i] * gl[i] + upd[i][CHUNK:]
            refs[d][6][sl, hl * 2 * hd:(hl + 1) * 2 * hd] = ws[i][CHUNK:] + upd[i][:CHUNK]

    gens = [chunk_stages(c) for c in range(cpb)]
    for tick in range(SCAN_STAGES + SCAN_SKEW * (cpb - 1)):
        for c in range(cpb):
            if 0 <= tick - SCAN_SKEW * c < SCAN_STAGES:
                next(gens[c], None)
    for i, (hl, d) in enumerate(pairs):
        s_scr[hl, d] = state[i]

    @pl.when(n == pl.num_programs(2) - 1)
    def _():
        sout_ref[...] = s_scr[...]


GDN_HPB = 2


def _gdn_scan(qn, kn, kt, v, gates, s0):
    b, t, _ = qn.shape
    rep = V_HEADS // QK_HEADS
    assert rep == 2
    nch = 2 * 2 * rep
    hpb = GDN_HPB
    g6 = gates[..., :4 * V_HEADS].reshape(b, t, 2, 2, QK_HEADS, rep)
    cols = g6.transpose(0, 4, 1, 2, 3, 5).reshape(b, QK_HEADS, t, nch)
    rows = (cols.reshape(b, QK_HEADS, t // CHUNK, CHUNK, 2, 2, rep).transpose(0, 1, 2, 4, 5, 6, 3)
            .reshape(b, QK_HEADS, t // CHUNK, 4, rep * CHUNK))
    blk = _pick(t, SCAN_CHUNKS * CHUNK, CHUNK)
    cpb = blk // CHUNK
    nb = t // blk
    fwd = lambda n: n
    bwd = lambda n: nb - 1 - n

    def specs(order):
        return [pl.BlockSpec((None, blk, hpb * HEAD_DIM), lambda bi, h, n: (bi, order(n), h)),
                pl.BlockSpec((None, blk, hpb * HEAD_DIM), lambda bi, h, n: (bi, order(n), h)),
                pl.BlockSpec((None, hpb, cpb, HEAD_DIM, rep * CHUNK), lambda bi, h, n: (bi, h, order(n), 0, 0)),
                pl.BlockSpec((None, blk, hpb * rep * HEAD_DIM), lambda bi, h, n: (bi, order(n), h)),
                pl.BlockSpec((None, hpb, blk, nch), lambda bi, h, n: (bi, h, order(n), 0)),
                pl.BlockSpec((None, hpb, cpb, 4, rep * CHUNK), lambda bi, h, n: (bi, h, order(n), 0, 0))]

    state_block = (None, hpb, 2, HEAD_DIM, rep * HEAD_DIM)
    state_spec = pl.BlockSpec(state_block, lambda bi, h, n: (bi, h, 0, 0, 0))
    o_shape = jax.ShapeDtypeStruct((b, t, V_HEADS * HEAD_DIM), F32)
    return pl.pallas_call(
        functools.partial(_gdn_scan_kernel, cpb=cpb, hpb=hpb),
        grid=(b, QK_HEADS // hpb, nb),
        in_specs=specs(fwd) + specs(bwd) + [state_spec],
        out_specs=[pl.BlockSpec((None, blk, hpb * rep * HEAD_DIM), lambda bi, h, n: (bi, fwd(n), h)),
                   pl.BlockSpec((None, blk, hpb * rep * HEAD_DIM), lambda bi, h, n: (bi, bwd(n), h)),
                   state_spec],
        out_shape=[o_shape, o_shape, jax.ShapeDtypeStruct(s0.shape, F32)],
        scratch_shapes=[pltpu.VMEM(state_block[1:], F32)],
        compiler_params=_cparams(("parallel", "parallel", "arbitrary")),
        name="gdn_scan",
    )(qn, kn, kt, v, cols, rows, qn, kn, kt, v, cols, rows, s0)


def _lru_scan_kernel(fc_ref, fp_ref, fn_ref, bc_ref, bp_ref, bn_ref, cw_ref, cb_ref, wr_ref, br_ref,
                     wi_ref, bi_ref, lam_ref, h0_ref, hf_ref, hb_ref, hfin_ref, ext, carry, *, tm):
    i = pl.program_id(1)
    nt = pl.num_programs(1)

    @pl.when(i == 0)
    def _():
        carry[...] = h0_ref[...]

    width = cw_ref.shape[1]
    bs = width // LRU_BLOCKS
    row = lax.broadcasted_iota(jnp.int32, (tm, bs), 0)
    dirs = ((fc_ref, fp_ref, fn_ref, hf_ref, i), (bc_ref, bp_ref, bn_ref, hb_ref, nt - 1 - i))
    for d, (c_ref, p_ref, n_ref, o_ref, tile) in enumerate(dirs):
        _fill_ext(ext, c_ref, p_ref, n_ref, tile, nt, tm)
        for blk in range(LRU_BLOCKS):
            lanes = slice(blk * bs, (blk + 1) * bs)
            xc = _conv_taps(ext, cw_ref, lanes, tm) + cb_ref[:, lanes]
            xc16 = xc.astype(BF16)
            r = _sigmoid(_dot(xc16, wr_ref[d, blk]) + br_ref[d:d + 1, lanes])
            gi = _sigmoid(_dot(xc16, wi_ref[d, blk]) + bi_ref[d:d + 1, lanes])
            log_a = -RG_C * r * _softplus(-lam_ref[d:d + 1, lanes])
            a = jnp.exp(log_a)
            bv = jnp.sqrt(1.0 - jnp.exp(2.0 * log_a)) * (gi * xc)
            s = 1
            while s < tm:
                shift = s if d == 0 else tm - s
                keep = (row >= s) if d == 0 else (row < tm - s)
                a_prev = jnp.where(keep, pltpu.roll(a, shift, 0), 1.0)
                b_prev = jnp.where(keep, pltpu.roll(bv, shift, 0), 0.0)
                bv = a * b_prev + bv
                a = a * a_prev
                s *= 2
            h = a * carry[d, 0:1, lanes] + bv
            o_ref[:, lanes] = h
            carry[d, 0:1, lanes] = h[tm - 1:tm, :] if d == 0 else h[0:1, :]

    @pl.when(i == nt - 1)
    def _():
        hfin_ref[...] = carry[...]


def _lru_scan(proj, conv_w, conv_b, w_r, b_r, w_i, b_i, lam, h0):
    b, t, two_w = proj.shape
    width = two_w // 2
    tm = _pick(t, 256, CHUNK)
    nt = t // tm
    whole = lambda a: pl.BlockSpec(a.shape, lambda bi, i: (0,) * a.ndim)
    state_spec = pl.BlockSpec((None, 2, 1, width), lambda bi, i: (bi, 0, 0, 0))
    consts = [conv_w, conv_b.reshape(1, width), w_r, b_r, w_i, b_i, lam]
    h_shape = jax.ShapeDtypeStruct((b, t, width), F32)
    return pl.pallas_call(
        functools.partial(_lru_scan_kernel, tm=tm),
        grid=(b, nt),
        in_specs=(_halo_specs(tm, t, width, 0, lambda bi, i: (bi, i))
                  + _halo_specs(tm, t, width, 0, lambda bi, i: (bi, nt - 1 - i))
                  + [whole(a) for a in consts] + [state_spec]),
        out_specs=[pl.BlockSpec((None, tm, width), lambda bi, i: (bi, i, 0)),
                   pl.BlockSpec((None, tm, width), lambda bi, i: (bi, nt - 1 - i, 0)),
                   state_spec],
        out_shape=[h_shape, h_shape, jax.ShapeDtypeStruct(h0.shape, F32)],
        scratch_shapes=[pltpu.VMEM((tm + 2 * HALO, width), F32),
                        pltpu.VMEM((2, 1, width), F32)],
        compiler_params=_cparams(("parallel", "arbitrary")),
        name="lru_scan",
    )(proj, proj, proj, proj, proj, proj, *consts, h0)


def _residual_store(xo_ref, x_ref, y, gate, fg_ref, nw, d, rows):
    for k in range(nw):
        lanes = slice(k * d, (k + 1) * d) if nw > 1 else slice(None)
        yk = y[k * rows:(k + 1) * rows] if nw > 1 else y
        xn = x_ref[:, lanes] + gate * yk
        if fg_ref is not None:
            xn = xn * lax.rsqrt(jnp.mean(xn * xn, axis=-1, keepdims=True) + EPS) * fg_ref[...]
        xo_ref[:, lanes] = xn


def _gdn_out_kernel(*refs, nw, d, rows, final):
    of_ref, ob_ref, z_ref, ng_ref, w_ref, x_ref, gate_ref = refs[:7]
    fg_ref = refs[7] if final else None
    xo_ref = refs[-1]
    parts = []
    for h in range(V_HEADS):
        lanes = slice(h * HEAD_DIM, (h + 1) * HEAD_DIM)
        o = of_ref[:, lanes] + ob_ref[:, lanes]
        o = o * lax.rsqrt(jnp.mean(o * o, axis=-1, keepdims=True) + EPS) * ng_ref[...]
        z = z_ref[:, lanes]
        parts.append((o * (z * _sigmoid(z))).astype(BF16))
    y = _dot(jnp.concatenate(parts, axis=1), w_ref[...])
    _residual_store(xo_ref, x_ref, y, gate_ref[...], fg_ref, nw, d, rows)


def _lru_out_kernel(*refs, nw, d, rows, final):
    hf_ref, hb_ref, gt_ref, w_ref, x_ref, gate_ref = refs[:6]
    fg_ref = refs[6] if final else None
    xo_ref = refs[-1]
    gt = gt_ref[...]
    hg = ((hf_ref[...] + hb_ref[...]) * (gt * _sigmoid(gt))).astype(BF16)
    y = _dot(hg, w_ref[...])
    _residual_store(xo_ref, x_ref, y, gate_ref[...], fg_ref, nw, d, rows)


def _out_proj(kind, branch, gate_src, gate_col, extra, w_out, x, gate, final_g, *, col_major):
    b, t, d = x.shape
    wb = branch[0].shape[-1]
    tm, nw, view, xblock, xmap = _stream_tiling(t, d, col_major, 256)
    tile = lambda col: pl.BlockSpec((None, tm, wb), lambda bi, i: (bi, i, col))
    in_specs = [tile(0), tile(0), tile(gate_col)]
    args = [branch[0], branch[1], gate_src]
    if kind == "gdn":
        in_specs.append(pl.BlockSpec((1, HEAD_DIM), lambda bi, i: (0, 0)))
        args.append(extra.reshape(1, HEAD_DIM))
        body = _gdn_out_kernel
    else:
        body = _lru_out_kernel
    in_specs += [pl.BlockSpec(w_out.shape, lambda bi, i: (0, 0)),
                 pl.BlockSpec(xblock, lambda bi, i: xmap(bi, i)),
                 pl.BlockSpec((None, 1, d), lambda bi, i: (bi, 0, 0))]
    args += [w_out, view(x), gate]
    final = final_g is not None
    if final:
        in_specs.append(pl.BlockSpec((1, d), lambda bi, i: (0, 0)))
        args.append(final_g.reshape(1, d))
    xv = view(x)
    out = pl.pallas_call(
        functools.partial(body, nw=nw, d=d, rows=tm // nw, final=final),
        grid=(b, t // tm),
        in_specs=in_specs,
        out_specs=pl.BlockSpec(xblock, lambda bi, i: xmap(bi, i)),
        out_shape=jax.ShapeDtypeStruct(xv.shape, F32),
        compiler_params=_cparams(("parallel", "parallel")),
        name=kind + "_out",
    )(*args)
    return out.reshape(b, t, d)


def _gdn_layer(streams, mods, norm_g, w_in, conv_w, a_log, dt_bias, head_g, w_out, *, col_major, final_g,
               update_ctx):
    qkvz_w = 2 * QK_HEADS * HEAD_DIM + 2 * V_HEADS * HEAD_DIM
    w_main = w_in[:, :qkvz_w].astype(BF16)
    w_ab = jnp.pad(w_in[:, qkvz_w:], ((0, 0), (0, LANES - (w_in.shape[1] - qkvz_w)))).astype(BF16)
    w_out16 = w_out.astype(BF16)
    state = None
    outs = []
    for (x, cm, fg, need_y), (shift, scale, gate) in zip(
            ((streams[0], False, None, update_ctx), (streams[1], col_major, final_g, True)), mods):
        b = x.shape[0]
        if state is None:
            state = jnp.zeros((b, QK_HEADS, 2, HEAD_DIM, V_HEADS // QK_HEADS * HEAD_DIM), F32)
        proj, ab = _in_proj(x, norm_g, shift, scale, w_main, w_ab, col_major=cm)
        gates = _gdn_gates(ab, a_log, dt_bias)
        qn, kn, v, kt = _gdn_prep(proj, conv_w)
        o_f, o_b, state = _gdn_scan(qn, kn, kt, v, gates, state)
        if need_y:
            x = _out_proj("gdn", (o_f, o_b), proj, 2, head_g, w_out16, x, gate, fg, col_major=cm)
        outs.append(x)
    return outs


def _lru_layer(streams, mods, norm_g, w_in, conv_w, conv_b, w_r, b_r, w_i, b_i, lam, w_out, *, col_major,
               final_g, update_ctx):
    w_in16 = w_in.astype(BF16)
    w_r16 = w_r.astype(BF16)
    w_i16 = w_i.astype(BF16)
    w_out16 = w_out.astype(BF16)
    state = None
    outs = []
    for (x, cm, fg, need_y), (shift, scale, gate) in zip(
            ((streams[0], False, None, update_ctx), (streams[1], col_major, final_g, True)), mods):
        b = x.shape[0]
        if state is None:
            state = jnp.zeros((b, 2, 1, conv_w.shape[1]), F32)
        proj = _in_proj(x, norm_g, shift, scale, w_in16, None, col_major=cm)
        h_f, h_b, state = _lru_scan(proj, conv_w, conv_b, w_r16, b_r, w_i16, b_i, lam, state)
        if need_y:
            x = _out_proj("lru", (h_f, h_b), proj, 1, None, w_out16, x, gate, fg, col_major=cm)
        outs.append(x)
    return outs


def kernel(x, c, ctx, c_ctx, mod_w, mod_b, norm_g, gdn_w_in, gdn_conv, gdn_a_log, gdn_dt_bias, gdn_norm_g,
           gdn_w_out, lru_w_in, lru_conv_w, lru_conv_b, lru_w_r, lru_b_r, lru_w_i, lru_b_i, lru_lambda,
           lru_w_out, final_g):
    b, _, d = x.shape
    depth = mod_w.shape[0]
    assert b + 1 <= MOD_ROWS
    cond = jnp.concatenate([c, c_ctx[None], jnp.zeros((MOD_ROWS - b - 1, d), F32)], axis=0)
    mod = _modulation(cond, mod_w, mod_b)
    for i in range(depth):
        lat = [mod[i, :b, k * d:(k + 1) * d].reshape(b, 1, d) for k in range(3)]
        cmod = [jnp.broadcast_to(mod[i, b, k * d:(k + 1) * d].reshape(1, 1, d), (b, 1, d)) for k in range(3)]
        col_major = (i + i // 2) % 2 == 1
        last = i == depth - 1
        j = i // 2
        common = dict(col_major=col_major, final_g=final_g if last else None, update_ctx=not last)
        if i % 2 == 0:
            ctx, x = _gdn_layer((ctx, x), (cmod, lat), norm_g[i], gdn_w_in[j], gdn_conv[j], gdn_a_log[j],
                                gdn_dt_bias[j], gdn_norm_g[j], gdn_w_out[j], **common)
        else:
            ctx, x = _lru_layer((ctx, x), (cmod, lat), norm_g[i], lru_w_in[j], lru_conv_w[j], lru_conv_b[j],
                                lru_w_r[j], lru_b_r[j], lru_w_i[j], lru_b_i[j], lru_lambda[j], lru_w_out[j],
                                **common)
    return x
```

```python
import functools

import jax
import jax.numpy as jnp
from jax import lax
from jax.experimental import pallas as pl
from jax.experimental.pallas import tpu as pltpu

F32 = jnp.float32
BF16 = jnp.bfloat16

EPS = 1e-6
GRID_W = 64
HEAD_DIM = 128
QK_HEADS = 8
V_HEADS = 16
CHUNK = 64
CONV_W = 4
CONV_LEFT = 2
HALO = 8
HALO_BLK = 16
LRU_BLOCKS = 4
RG_C = 8.0
LANES = 128
SUB = 8
MOD_ROWS = 8

VMEM_LIMIT = 56 * 1024 * 1024


def _cparams(sem):
    return pltpu.CompilerParams(dimension_semantics=sem, vmem_limit_bytes=VMEM_LIMIT)


def _pick(total, target, mult):
    best = None
    for cand in range(mult, min(total, target) + 1, mult):
        if total % cand == 0:
            best = cand
    assert best is not None, (total, target, mult)
    return best


def _sigmoid(x):
    return 1.0 / (1.0 + jnp.exp(-x))


def _softplus(x):
    return jnp.maximum(x, 0.0) + jnp.log(1.0 + jnp.exp(-jnp.abs(x)))


def _dot(a, b):
    return jnp.dot(a, b, preferred_element_type=F32)


def _dot_nt(a, b):
    return lax.dot_general(a, b, (((1,), (1,)), ((), ())), preferred_element_type=F32)


def _dot_tn(a, b):
    return lax.dot_general(a, b, (((0,), (0,)), ((), ())), preferred_element_type=F32)


def _mod_kernel(c_ref, w_ref, b_ref, o_ref):
    c = c_ref[...]
    sc = c * _sigmoid(c)
    o_ref[...] = jnp.dot(sc, w_ref[...], preferred_element_type=F32,
                         precision=lax.Precision.HIGHEST) + b_ref[...]


def _modulation(cond, mod_w, mod_b):
    depth, d, n3 = mod_w.shape
    return pl.pallas_call(
        _mod_kernel,
        grid=(depth, n3 // d),
        in_specs=[pl.BlockSpec((MOD_ROWS, d), lambda i, j: (0, 0)),
                  pl.BlockSpec((None, d, d), lambda i, j: (i, 0, j)),
                  pl.BlockSpec((None, 1, d), lambda i, j: (i, 0, j))],
        out_specs=pl.BlockSpec((None, MOD_ROWS, d), lambda i, j: (i, 0, j)),
        out_shape=jax.ShapeDtypeStruct((depth, MOD_ROWS, n3), F32),
        compiler_params=_cparams(("parallel", "parallel")),
        name="modulation",
    )(cond, mod_w, mod_b.reshape(depth, 1, n3))


def _stream_tiling(t, d, col_major, target):
    if not col_major:
        tm = _pick(t, target, CHUNK)
        return tm, 1, (lambda a: a), (None, tm, d), (lambda b, i: (b, i, 0))
    rows = t // GRID_W
    assert rows * GRID_W == t and rows % 8 == 0
    nw = _pick(GRID_W, max(target // rows, 1), 1)
    tm = rows * nw
    return (tm, nw, (lambda a: a.reshape(a.shape[0], rows, GRID_W * d)),
            (None, rows, nw * d), (lambda b, i: (b, 0, i)))


def _load_stream_tile(x_ref, nw, d):
    if nw == 1:
        return x_ref[...]
    return jnp.concatenate([x_ref[:, k * d:(k + 1) * d] for k in range(nw)], axis=0)


def _in_proj_kernel(*refs, nw, d, has_extra):
    if has_extra:
        x_ref, g_ref, sh_ref, sc_ref, w_ref, we_ref, o_ref, oe_ref, h_scr = refs
    else:
        x_ref, g_ref, sh_ref, sc_ref, w_ref, o_ref, h_scr = refs

    @pl.when(pl.program_id(2) == 0)
    def _():
        x = _load_stream_tile(x_ref, nw, d)
        y = x * lax.rsqrt(jnp.mean(x * x, axis=-1, keepdims=True) + EPS) * g_ref[...]
        h = (y * (1.0 + sc_ref[...]) + sh_ref[...]).astype(BF16)
        h_scr[...] = h
        if has_extra:
            oe_ref[...] = _dot(h, we_ref[...])

    o_ref[...] = _dot(h_scr[...], w_ref[...]).astype(BF16)


def _in_proj(x, norm_g, shift, scale, w, w_extra, *, col_major):
    b, t, d = x.shape
    n = w.shape[1]
    tn = _pick(n, 1024, LANES)
    tm, nw, view, xblock, xmap = _stream_tiling(t, d, col_major, 1024)
    has_extra = w_extra is not None
    vec = pl.BlockSpec((None, 1, d), lambda bi, i, j: (bi, 0, 0))
    in_specs = [pl.BlockSpec(xblock, lambda bi, i, j: xmap(bi, i)),
                pl.BlockSpec((1, d), lambda bi, i, j: (0, 0)), vec, vec,
                pl.BlockSpec((d, tn), lambda bi, i, j: (0, j))]
    out_specs = [pl.BlockSpec((None, tm, tn), lambda bi, i, j: (bi, i, j))]
    out_shape = [jax.ShapeDtypeStruct((b, t, n), BF16)]
    args = [view(x), norm_g.reshape(1, d), shift, scale, w]
    if has_extra:
        in_specs.append(pl.BlockSpec((d, LANES), lambda bi, i, j: (0, 0)))
        out_specs.append(pl.BlockSpec((None, tm, LANES), lambda bi, i, j: (bi, i, 0)))
        out_shape.append(jax.ShapeDtypeStruct((b, t, LANES), F32))
        args.append(w_extra)
    outs = pl.pallas_call(
        functools.partial(_in_proj_kernel, nw=nw, d=d, has_extra=has_extra),
        grid=(b, t // tm, n // tn),
        in_specs=in_specs, out_specs=out_specs, out_shape=out_shape,
        scratch_shapes=[pltpu.VMEM((tm, d), BF16)],
        compiler_params=_cparams(("parallel", "parallel", "arbitrary")),
        name="in_proj",
    )(*args)
    return outs if has_extra else outs[0]


def _gdn_gate_kernel(ab_ref, alog_ref, dtb_ref, o_ref, *, tm):
    x = ab_ref[...]
    g = -jnp.exp(alog_ref[...]) * _softplus(x + dtb_ref[...])
    beta = _sigmoid(x)
    ri = lax.broadcasted_iota(jnp.int32, (CHUNK, CHUNK), 0)
    ci = lax.broadcasted_iota(jnp.int32, (CHUNK, CHUNK), 1)
    tril = (ri >= ci).astype(F32)
    triu = (ri <= ci).astype(F32)
    group = lax.broadcasted_iota(jnp.int32, (CHUNK, LANES), 1) // V_HEADS
    for c in range(tm // CHUNK):
        rows = slice(c * CHUNK, (c + 1) * CHUNK)
        gc = g[rows]
        pre = jnp.dot(tril, gc, preferred_element_type=F32, precision=lax.Precision.HIGHEST)
        suf = jnp.dot(triu, gc, preferred_element_type=F32, precision=lax.Precision.HIGHEST)
        o_ref[rows, :] = jnp.where(group == 0, pre, jnp.where(group == 2, suf, beta[rows]))


def _gdn_gates(ab, a_log, dt_bias):
    b, t, _ = ab.shape
    tm = _pick(t, 512, CHUNK)
    zeros = jnp.zeros((2, V_HEADS), F32)
    lane_vec = lambda p: jnp.pad(jnp.stack([p, zeros], axis=1).reshape(1, 4 * V_HEADS),
                                 ((0, 0), (0, LANES - 4 * V_HEADS)))
    return pl.pallas_call(
        functools.partial(_gdn_gate_kernel, tm=tm),
        grid=(b, t // tm),
        in_specs=[pl.BlockSpec((None, tm, LANES), lambda bi, i: (bi, i, 0)),
                  pl.BlockSpec((1, LANES), lambda bi, i: (0, 0)),
                  pl.BlockSpec((1, LANES), lambda bi, i: (0, 0))],
        out_specs=pl.BlockSpec((None, tm, LANES), lambda bi, i: (bi, i, 0)),
        out_shape=jax.ShapeDtypeStruct((b, t, LANES), F32),
        compiler_params=_cparams(("parallel", "parallel")),
        name="gdn_gates",
    )(ab, lane_vec(a_log), lane_vec(dt_bias))


def _halo_specs(tm, t, width, col_block, tile_of):
    per = tm // HALO_BLK
    last = t // HALO_BLK - 1

    def cur(*g):
        bi, i = tile_of(*g)
        return (bi, i, col_block)

    def prev(*g):
        bi, i = tile_of(*g)
        return (bi, jnp.maximum(i * per - 1, 0), col_block)

    def nxt(*g):
        bi, i = tile_of(*g)
        return (bi, jnp.minimum((i + 1) * per, last), col_block)

    return [pl.BlockSpec((None, tm, width), cur),
            pl.BlockSpec((None, HALO_BLK, width), prev),
            pl.BlockSpec((None, HALO_BLK, width), nxt)]


def _fill_ext(ext, cur_ref, prev_ref, next_ref, i, nt, tm):
    ext[0:HALO, :] = jnp.where(i > 0, prev_ref[HALO_BLK - HALO:, :].astype(F32), 0.0)
    ext[HALO:HALO + tm, :] = cur_ref[...].astype(F32)
    ext[HALO + tm:, :] = jnp.where(i < nt - 1, next_ref[0:HALO, :].astype(F32), 0.0)


def _conv_taps(ext, w_ref, lanes, tm):
    acc = None
    for j in range(CONV_W):
        r0 = HALO - CONV_LEFT + j
        term = ext[r0:r0 + tm, lanes] * w_ref[j:j + 1, lanes]
        acc = term if acc is None else acc + term
    return acc


def _gdn_prep_kernel(cur_ref, prev_ref, next_ref, w_ref, q_ref, k_ref, v_ref, kt_ref, ext, *, tm):
    _fill_ext(ext, cur_ref, prev_ref, next_ref, pl.program_id(1), pl.num_programs(1), tm)
    qk_w = QK_HEADS * HEAD_DIM
    for h in range(2 * QK_HEADS + V_HEADS):
        lanes = slice(h * HEAD_DIM, (h + 1) * HEAD_DIM)
        y = _conv_taps(ext, w_ref, lanes, tm)
        y = y * _sigmoid(y)
        if h < 2 * QK_HEADS:
            y = y * lax.rsqrt(jnp.sum(y * y, axis=-1, keepdims=True) + EPS)
            if h < QK_HEADS:
                q_ref[:, lanes] = (y * (HEAD_DIM ** -0.5)).astype(BF16)
            else:
                kh = h - QK_HEADS
                k_ref[:, kh * HEAD_DIM:(kh + 1) * HEAD_DIM] = y.astype(BF16)
                for c in range(tm // CHUNK):
                    yc = y[c * CHUNK:(c + 1) * CHUNK]
                    kt_ref[kh, c] = jnp.concatenate([yc, yc], axis=0).T.astype(BF16)
        else:
            v_ref[:, h * HEAD_DIM - 2 * qk_w:(h + 1) * HEAD_DIM - 2 * qk_w] = y.astype(BF16)


def _gdn_prep(proj, conv_w):
    b, t, _ = proj.shape
    qk_w = QK_HEADS * HEAD_DIM
    v_w = V_HEADS * HEAD_DIM
    qkv_w = 2 * qk_w + v_w
    tm = _pick(t, 256, CHUNK)
    tile = lambda width: pl.BlockSpec((None, tm, width), lambda bi, i: (bi, i, 0))
    kt_block = (None, QK_HEADS, tm // CHUNK, HEAD_DIM, 2 * CHUNK)
    return pl.pallas_call(
        functools.partial(_gdn_prep_kernel, tm=tm),
        grid=(b, t // tm),
        in_specs=_halo_specs(tm, t, qkv_w, 0, lambda bi, i: (bi, i)) + [
            pl.BlockSpec((CONV_W, qkv_w), lambda bi, i: (0, 0))],
        out_specs=[tile(qk_w), tile(qk_w), tile(v_w),
                   pl.BlockSpec(kt_block, lambda bi, i: (bi, 0, i, 0, 0))],
        out_shape=[jax.ShapeDtypeStruct((b, t, qk_w), BF16),
                   jax.ShapeDtypeStruct((b, t, qk_w), BF16),
                   jax.ShapeDtypeStruct((b, t, v_w), BF16),
                   jax.ShapeDtypeStruct((b, QK_HEADS, t // CHUNK, HEAD_DIM, 2 * CHUNK), BF16)],
        scratch_shapes=[pltpu.VMEM((tm + 2 * HALO, qkv_w), F32)],
        compiler_params=_cparams(("parallel", "parallel")),
        name="gdn_prep",
    )(proj, proj, proj, conv_w)


INV_BASE = 8


def _block_diag2(pair, first):
    return jnp.concatenate([jnp.where(first, pair, 0.0), jnp.where(first, 0.0, pair)], axis=0).astype(BF16)


SCAN_STAGES = 13
SCAN_SKEW = 2
SCAN_CHUNKS = 16


def _gdn_scan_kernel(qf_ref, kf_ref, tf_ref, vf_ref, cf_ref, rf_ref, qb_ref, kb_ref, tb_ref, vb_ref, cb_ref,
                     rb_ref, s0_ref, of_ref, ob_ref, sout_ref, s_scr, *, cpb, hpb):
    n = pl.program_id(2)

    @pl.when(n == 0)
    def _():
        s_scr[...] = s0_ref[...]

    ri = lax.broadcasted_iota(jnp.int32, (CHUNK, 2 * CHUNK), 0)
    lane = lax.broadcasted_iota(jnp.int32, (CHUNK, 2 * CHUNK), 1)
    first = lane < CHUNK
    first_row = first[0:1]
    ci = jnp.where(first, lane, lane - CHUNK)
    eye = (ri == ci).astype(F32)
    incl = (ri >= ci, ri <= ci)
    strict = (ri > ci, ri < ci)
    same_block = lambda size: (ri // size) == (ci // size)
    refs = ((qf_ref, kf_ref, tf_ref, vf_ref, cf_ref, rf_ref, of_ref),
            (qb_ref, kb_ref, tb_ref, vb_ref, cb_ref, rb_ref, ob_ref))
    pairs = [(hl, d) for hl in range(hpb) for d in range(2)]
    hd = HEAD_DIM
    bd = lambda xs: [_block_diag2(x, first) for x in xs]
    zero_s = jnp.zeros((hd, hd), BF16)
    zero_v = jnp.zeros((CHUNK, hd), BF16)
    state = [s_scr[hl, d] for hl, d in pairs]

    def chunk_stages(c):
        probs = []
        for hl, d in pairs:
            cc = c if d == 0 else cpb - 1 - c
            probs.append((hl, d, cc, slice(cc * CHUNK, (cc + 1) * CHUNK)))
        idx = range(len(probs))
        gram = [_dot(jnp.concatenate([refs[d][0][sl, hl * hd:(hl + 1) * hd],
                                      refs[d][1][sl, hl * hd:(hl + 1) * hd]], axis=0), refs[d][2][hl, cc])
                for hl, d, cc, sl in probs]
        yield
        lmat, lhs2, qd, scale_u, scale_w, gl = [], [], [], [], [], []
        for j, (hl, d, cc, sl) in enumerate(probs):
            q_ref, k_ref, t_ref, v_ref, c_ref, r_ref, _ = refs[d]
            cols = c_ref[hl, sl, :]
            rows = r_ref[hl, cc]
            grow = rows[2 * d:2 * d + 1, :]
            brow = rows[2 * d + 1:2 * d + 2, :]
            gcols = [cols[:, d * 4 + vh:d * 4 + vh + 1] for vh in range(2)]
            bcols = [cols[:, d * 4 + 2 + vh:d * 4 + 3 + vh] for vh in range(2)]
            gtots = [g[CHUNK - 1:CHUNK, :] if d == 0 else g[0:1, :] for g in gcols]
            gpair = jnp.where(first, gcols[0], gcols[1])
            bpair = jnp.where(first, bcols[0], bcols[1])
            decay = jnp.where(incl[d], jnp.exp(jnp.where(incl[d], gpair - grow, 0.0)), 0.0)
            lmat.append(jnp.where(strict[d], gram[j][CHUNK:] * bpair * decay, 0.0))
            qkm = gram[j][:CHUNK] * decay
            kdt = t_ref[hl, cc].astype(F32) * jnp.exp(jnp.where(first_row, gtots[0], gtots[1]) - grow)
            lhs2.append(jnp.concatenate([qkm, kdt], axis=0).astype(BF16))
            q = q_ref[sl, hl * hd:(hl + 1) * hd].astype(F32)
            qd.append(jnp.concatenate([q * jnp.exp(g) for g in gcols], axis=1))
            scale_u.append(brow)
            scale_w.append(brow * jnp.exp(grow))
            gl.append(jnp.concatenate([jnp.broadcast_to(jnp.exp(g), (1, hd)) for g in gtots], axis=1))
        base = same_block(INV_BASE)
        p = [-jnp.where(base, l, 0.0) for l in lmat]
        p2 = [_dot(p[i].astype(BF16), m) for i, m in enumerate(bd(p))]
        yield
        p2_bd = bd(p2)
        p4 = [_dot(p2[i].astype(BF16), p2_bd[i]) for i in idx]
        tinv = [eye + x for x in p]
        tinv = [tinv[i] + _dot(tinv[i].astype(BF16), p2_bd[i]) for i in idx]
        yield
        tinv = [tinv[i] + _dot(tinv[i].astype(BF16), m) for i, m in enumerate(bd(p4))]
        yield
        size = INV_BASE
        while size < CHUNK:
            off_diag = same_block(2 * size) & jnp.logical_not(same_block(size))
            a16 = [jnp.where(off_diag, l, 0.0).astype(BF16) for l in lmat]
            at = [_dot(a16[i], m) for i, m in enumerate(bd(tinv))]
            yield
            tinv = [tinv[i] - _dot(tinv[i].astype(BF16), m) for i, m in enumerate(bd(at))]
            yield
            size *= 2
        u, w = [], []
        for j, (hl, d, cc, sl) in enumerate(probs):
            k16 = refs[d][1][sl, hl * hd:(hl + 1) * hd]
            v16 = [refs[d][3][sl, (hl * 2 + vh) * hd:(hl * 2 + vh + 1) * hd] for vh in range(2)]
            v_bd16 = jnp.concatenate([jnp.concatenate([v16[0], zero_v], axis=1),
                                      jnp.concatenate([zero_v, v16[1]], axis=1)], axis=0)
            k_bd16 = jnp.concatenate([jnp.concatenate([k16, zero_v], axis=1),
                                      jnp.concatenate([zero_v, k16], axis=1)], axis=0)
            u.append(_dot((tinv[j] * scale_u[j]).astype(BF16), v_bd16))
            w.append(_dot((tinv[j] * scale_w[j]).astype(BF16), k_bd16))
        yield
        wq16 = [jnp.concatenate([w[j], qd[j]], axis=0).astype(BF16) for j in idx]
        s16 = [x.astype(BF16) for x in state]
        s_bd = [jnp.concatenate([jnp.concatenate([x[:, :hd], zero_s], axis=1),
                                 jnp.concatenate([zero_s, x[:, hd:]], axis=1)], axis=0) for x in s16]
        ws = [_dot(wq16[i], s_bd[i]) for i in idx]
        yield
        vnew16 = [(u[i] - ws[i][:CHUNK]).astype(BF16) for i in idx]
        v_bd = [jnp.concatenate([jnp.concatenate([x[:, :hd], zero_v], axis=1),
                                 jnp.concatenate([zero_v, x[:, hd:]], axis=1)], axis=0) for x in vnew16]
        upd = [_dot(lhs2[i], v_bd[i]) for i in idx]
        for i, (hl, d, cc, sl) in enumerate(probs):
            state[i] = state[i] * gl[i] + upd[i][CHUNK:]
            refs[d][6][sl, hl * 2 * hd:(hl + 1) * 2 * hd] = ws[i][CHUNK:] + upd[i][:CHUNK]

    gens = [chunk_stages(c) for c in range(cpb)]
    for tick in range(SCAN_STAGES + SCAN_SKEW * (cpb - 1)):
        for c in range(cpb):
            if 0 <= tick - SCAN_SKEW * c < SCAN_STAGES:
                next(gens[c], None)
    for i, (hl, d) in enumerate(pairs):
        s_scr[hl, d] = state[i]

    @pl.when(n == pl.num_programs(2) - 1)
    def _():
        sout_ref[...] = s_scr[...]


GDN_HPB = 2


def _gdn_scan(qn, kn, kt, v, gates, s0):
    b, t, _ = qn.shape
    rep = V_HEADS // QK_HEADS
    assert rep == 2
    nch = 2 * 2 * rep
    hpb = GDN_HPB
    g6 = gates[..., :4 * V_HEADS].reshape(b, t, 2, 2, QK_HEADS, rep)
    cols = g6.transpose(0, 4, 1, 2, 3, 5).reshape(b, QK_HEADS, t, nch)
    rows = (cols.reshape(b, QK_HEADS, t // CHUNK, CHUNK, 2, 2, rep).transpose(0, 1, 2, 4, 5, 6, 3)
            .reshape(b, QK_HEADS, t // CHUNK, 4, rep * CHUNK))
    blk = _pick(t, SCAN_CHUNKS * CHUNK, CHUNK)
    cpb = blk // CHUNK
    nb = t // blk
    fwd = lambda n: n
    bwd = lambda n: nb - 1 - n

    def specs(order):
        return [pl.BlockSpec((None, blk, hpb * HEAD_DIM), lambda bi, h, n: (bi, order(n), h)),
                pl.BlockSpec((None, blk, hpb * HEAD_DIM), lambda bi, h, n: (bi, order(n), h)),
                pl.BlockSpec((None, hpb, cpb, HEAD_DIM, rep * CHUNK), lambda bi, h, n: (bi, h, order(n), 0, 0)),
                pl.BlockSpec((None, blk, hpb * rep * HEAD_DIM), lambda bi, h, n: (bi, order(n), h)),
                pl.BlockSpec((None, hpb, blk, nch), lambda bi, h, n: (bi, h, order(n), 0)),
                pl.BlockSpec((None, hpb, cpb, 4, rep * CHUNK), lambda bi, h, n: (bi, h, order(n), 0, 0))]

    state_block = (None, hpb, 2, HEAD_DIM, rep * HEAD_DIM)
    state_spec = pl.BlockSpec(state_block, lambda bi, h, n: (bi, h, 0, 0, 0))
    o_shape = jax.ShapeDtypeStruct((b, t, V_HEADS * HEAD_DIM), F32)
    return pl.pallas_call(
        functools.partial(_gdn_scan_kernel, cpb=cpb, hpb=hpb),
        grid=(b, QK_HEADS // hpb, nb),
        in_specs=specs(fwd) + specs(bwd) + [state_spec],
        out_specs=[pl.BlockSpec((None, blk, hpb * rep * HEAD_DIM), lambda bi, h, n: (bi, fwd(n), h)),
                   pl.BlockSpec((None, blk, hpb * rep * HEAD_DIM), lambda bi, h, n: (bi, bwd(n), h)),
                   state_spec],
        out_shape=[o_shape, o_shape, jax.ShapeDtypeStruct(s0.shape, F32)],
        scratch_shapes=[pltpu.VMEM(state_block[1:], F32)],
        compiler_params=_cparams(("parallel", "parallel", "arbitrary")),
        name="gdn_scan",
    )(qn, kn, kt, v, cols, rows, qn, kn, kt, v, cols, rows, s0)


def _lru_scan_kernel(fc_ref, fp_ref, fn_ref, bc_ref, bp_ref, bn_ref, cw_ref, cb_ref, wr_ref, br_ref,
                     wi_ref, bi_ref, lam_ref, h0_ref, hf_ref, hb_ref, hfin_ref, ext, carry, *, tm):
    i = pl.program_id(1)
    nt = pl.num_programs(1)

    @pl.when(i == 0)
    def _():
        carry[...] = h0_ref[...]

    width = cw_ref.shape[1]
    bs = width // LRU_BLOCKS
    nv = tm // SUB
    tpb = bs // LANES
    sub = lax.broadcasted_iota(jnp.int32, (SUB, bs), 0)
    dirs = ((fc_ref, fp_ref, fn_ref, hf_ref, i), (bc_ref, bp_ref, bn_ref, hb_ref, nt - 1 - i))
    for d, (c_ref, p_ref, n_ref, o_ref, tile) in enumerate(dirs):
        for lt in range(width // LANES):
            lt_lanes = slice(lt * LANES, (lt + 1) * LANES)
            ext[lt, 0:HALO, :] = jnp.where(tile > 0, p_ref[HALO_BLK - HALO:, lt_lanes].astype(F32), 0.0)
            ext[lt, HALO:HALO + tm, :] = c_ref[:, lt_lanes].astype(F32)
            ext[lt, HALO + tm:, :] = jnp.where(tile < nt - 1, n_ref[0:HALO, lt_lanes].astype(F32), 0.0)
        for blk in range(LRU_BLOCKS):
            lanes = slice(blk * bs, (blk + 1) * bs)
            groups = []
            for v in range(nv):
                pieces = []
                for lt in range(blk * tpb, (blk + 1) * tpb):
                    acc = None
                    for j in range(CONV_W):
                        term = (ext[lt, pl.ds(HALO - CONV_LEFT + j + v, SUB, stride=nv), :]
                                * cw_ref[j:j + 1, lt * LANES:(lt + 1) * LANES])
                        acc = term if acc is None else acc + term
                    pieces.append(acc)
                groups.append(jnp.concatenate(pieces, axis=1))
            xc = jnp.concatenate(groups, axis=0) + cb_ref[:, lanes]
            xc16 = xc.astype(BF16)
            r = _sigmoid(_dot(xc16, wr_ref[d, blk]) + br_ref[d:d + 1, lanes])
            gi = _sigmoid(_dot(xc16, wi_ref[d, blk]) + bi_ref[d:d + 1, lanes])
            log_a = -RG_C * r * _softplus(-lam_ref[d:d + 1, lanes])
            a = jnp.exp(log_a)
            bv = jnp.sqrt(1.0 - jnp.exp(2.0 * log_a)) * (gi * xc)
            order = range(nv) if d == 0 else range(nv - 1, -1, -1)
            hs, acums = [None] * nv, [None] * nv
            h_run = a_run = None
            for v in order:
                av, bvv = a[v * SUB:(v + 1) * SUB], bv[v * SUB:(v + 1) * SUB]
                h_run = bvv if h_run is None else av * h_run + bvv
                a_run = av if a_run is None else av * a_run
                hs[v], acums[v] = h_run, a_run
            e_end, p_end = h_run, a_run
            s = 1
            while s < SUB:
                shift = s if d == 0 else SUB - s
                keep = (sub >= s) if d == 0 else (sub < SUB - s)
                p_prev = jnp.where(keep, pltpu.roll(p_end, shift, 0), 1.0)
                e_prev = jnp.where(keep, pltpu.roll(e_end, shift, 0), 0.0)
                e_end = p_end * e_prev + e_end
                p_end = p_end * p_prev
                s *= 2
            cin = jnp.broadcast_to(carry[d, 0:1, lanes], (SUB, bs))
            one = 1 if d == 0 else SUB - 1
            first_sub = (sub == 0) if d == 0 else (sub == SUB - 1)
            start = jnp.where(first_sub, cin, pltpu.roll(p_end, one, 0) * cin + pltpu.roll(e_end, one, 0))
            for v in range(nv):
                h = acums[v] * start + hs[v]
                for k in range(tpb):
                    o_ref[blk * tpb + k, pl.ds(v, SUB, stride=nv), :] = h[:, k * LANES:(k + 1) * LANES]
            last = SUB - 1 if d == 0 else 0
            carry[d, 0:1, lanes] = (p_end * cin + e_end)[last:last + 1, :]

    @pl.when(i == nt - 1)
    def _():
        hfin_ref[...] = carry[...]


def _lru_scan(proj, conv_w, conv_b, w_r, b_r, w_i, b_i, lam, h0):
    b, t, two_w = proj.shape
    width = two_w // 2
    tm = _pick(t, 256, CHUNK)
    nt = t // tm
    nlt = width // LANES
    whole = lambda a: pl.BlockSpec(a.shape, lambda bi, i: (0,) * a.ndim)
    state_spec = pl.BlockSpec((None, 2, 1, width), lambda bi, i: (bi, 0, 0, 0))
    consts = [conv_w, conv_b.reshape(1, width), w_r, b_r, w_i, b_i, lam]
    h_shape = jax.ShapeDtypeStruct((b, nlt, t, LANES), F32)
    return pl.pallas_call(
        functools.partial(_lru_scan_kernel, tm=tm),
        grid=(b, nt),
        in_specs=(_halo_specs(tm, t, width, 0, lambda bi, i: (bi, i))
                  + _halo_specs(tm, t, width, 0, lambda bi, i: (bi, nt - 1 - i))
                  + [whole(a) for a in consts] + [state_spec]),
        out_specs=[pl.BlockSpec((None, nlt, tm, LANES), lambda bi, i: (bi, 0, i, 0)),
                   pl.BlockSpec((None, nlt, tm, LANES), lambda bi, i: (bi, 0, nt - 1 - i, 0)),
                   state_spec],
        out_shape=[h_shape, h_shape, jax.ShapeDtypeStruct(h0.shape, F32)],
        scratch_shapes=[pltpu.VMEM((nlt, tm + 2 * HALO, LANES), F32),
                        pltpu.VMEM((2, 1, width), F32)],
        compiler_params=_cparams(("parallel", "arbitrary")),
        name="lru_scan",
    )(proj, proj, proj, proj, proj, proj, *consts, h0)


def _residual_store(xo_ref, x_ref, y, gate, fg_ref, nw, d, rows):
    for k in range(nw):
        lanes = slice(k * d, (k + 1) * d) if nw > 1 else slice(None)
        yk = y[k * rows:(k + 1) * rows] if nw > 1 else y
        xn = x_ref[:, lanes] + gate * yk
        if fg_ref is not None:
            xn = xn * lax.rsqrt(jnp.mean(xn * xn, axis=-1, keepdims=True) + EPS) * fg_ref[...]
        xo_ref[:, lanes] = xn


def _gdn_out_kernel(*refs, nw, d, rows, final):
    of_ref, ob_ref, z_ref, ng_ref, w_ref, x_ref, gate_ref = refs[:7]
    fg_ref = refs[7] if final else None
    xo_ref = refs[-1]
    parts = []
    for h in range(V_HEADS):
        lanes = slice(h * HEAD_DIM, (h + 1) * HEAD_DIM)
        o = of_ref[:, lanes] + ob_ref[:, lanes]
        o = o * lax.rsqrt(jnp.mean(o * o, axis=-1, keepdims=True) + EPS) * ng_ref[...]
        z = z_ref[:, lanes].astype(F32)
        parts.append((o * (z * _sigmoid(z))).astype(BF16))
    y = _dot(jnp.concatenate(parts, axis=1), w_ref[...])
    _residual_store(xo_ref, x_ref, y, gate_ref[...], fg_ref, nw, d, rows)


def _lru_out_kernel(*refs, nw, d, rows, final):
    hf_ref, hb_ref, gt_ref, w_ref, x_ref, gate_ref = refs[:6]
    fg_ref = refs[6] if final else None
    xo_ref = refs[-1]
    gt = gt_ref[...].astype(F32)
    h = jnp.concatenate([hf_ref[lt] + hb_ref[lt] for lt in range(hf_ref.shape[0])], axis=1)
    hg = (h * (gt * _sigmoid(gt))).astype(BF16)
    y = _dot(hg, w_ref[...])
    _residual_store(xo_ref, x_ref, y, gate_ref[...], fg_ref, nw, d, rows)


def _out_proj(kind, branch, gate_src, gate_col, extra, w_out, x, gate, final_g, *, col_major):
    b, t, d = x.shape
    wb = w_out.shape[0]
    tm, nw, view, xblock, xmap = _stream_tiling(t, d, col_major, 256)
    tile = lambda col: pl.BlockSpec((None, tm, wb), lambda bi, i: (bi, i, col))
    if kind == "gdn":
        in_specs = [tile(0), tile(0), tile(gate_col), pl.BlockSpec((1, HEAD_DIM), lambda bi, i: (0, 0))]
        args = [branch[0], branch[1], gate_src, extra.reshape(1, HEAD_DIM)]
        body = _gdn_out_kernel
    else:
        lane_tiles = pl.BlockSpec((None, wb // LANES, tm, LANES), lambda bi, i: (bi, 0, i, 0))
        in_specs = [lane_tiles, lane_tiles, tile(gate_col)]
        args = [branch[0], branch[1], gate_src]
        body = _lru_out_kernel
    in_specs += [pl.BlockSpec(w_out.shape, lambda bi, i: (0, 0)),
                 pl.BlockSpec(xblock, lambda bi, i: xmap(bi, i)),
                 pl.BlockSpec((None, 1, d), lambda bi, i: (bi, 0, 0))]
    args += [w_out, view(x), gate]
    final = final_g is not None
    if final:
        in_specs.append(pl.BlockSpec((1, d), lambda bi, i: (0, 0)))
        args.append(final_g.reshape(1, d))
    xv = view(x)
    out = pl.pallas_call(
        functools.partial(body, nw=nw, d=d, rows=tm // nw, final=final),
        grid=(b, t // tm),
        in_specs=in_specs,
        out_specs=pl.BlockSpec(xblock, lambda bi, i: xmap(bi, i)),
        out_shape=jax.ShapeDtypeStruct(xv.shape, F32),
        compiler_params=_cparams(("parallel", "parallel")),
        name=kind + "_out",
    )(*args)
    return out.reshape(b, t, d)


def _gdn_layer(streams, mods, norm_g, w_in, conv_w, a_log, dt_bias, head_g, w_out, *, col_major, final_g,
               update_ctx):
    qkvz_w = 2 * QK_HEADS * HEAD_DIM + 2 * V_HEADS * HEAD_DIM
    w_main = w_in[:, :qkvz_w].astype(BF16)
    w_ab = jnp.pad(w_in[:, qkvz_w:], ((0, 0), (0, LANES - (w_in.shape[1] - qkvz_w)))).astype(BF16)
    w_out16 = w_out.astype(BF16)
    state = None
    outs = []
    for (x, cm, fg, need_y), (shift, scale, gate) in zip(
            ((streams[0], False, None, update_ctx), (streams[1], col_major, final_g, True)), mods):
        b = x.shape[0]
        if state is None:
            state = jnp.zeros((b, QK_HEADS, 2, HEAD_DIM, V_HEADS // QK_HEADS * HEAD_DIM), F32)
        proj, ab = _in_proj(x, norm_g, shift, scale, w_main, w_ab, col_major=cm)
        gates = _gdn_gates(ab, a_log, dt_bias)
        qn, kn, v, kt = _gdn_prep(proj, conv_w)
        o_f, o_b, state = _gdn_scan(qn, kn, kt, v, gates, state)
        if need_y:
            x = _out_proj("gdn", (o_f, o_b), proj, 2, head_g, w_out16, x, gate, fg, col_major=cm)
        outs.append(x)
    return outs


def _lru_layer(streams, mods, norm_g, w_in, conv_w, conv_b, w_r, b_r, w_i, b_i, lam, w_out, *, col_major,
               final_g, update_ctx):
    w_in16 = w_in.astype(BF16)
    w_r16 = w_r.astype(BF16)
    w_i16 = w_i.astype(BF16)
    w_out16 = w_out.astype(BF16)
    state = None
    outs = []
    for (x, cm, fg, need_y), (shift, scale, gate) in zip(
            ((streams[0], False, None, update_ctx), (streams[1], col_major, final_g, True)), mods):
        b = x.shape[0]
        if state is None:
            state = jnp.zeros((b, 2, 1, conv_w.shape[1]), F32)
        proj = _in_proj(x, norm_g, shift, scale, w_in16, None, col_major=cm)
        h_f, h_b, state = _lru_scan(proj, conv_w, conv_b, w_r16, b_r, w_i16, b_i, lam, state)
        if need_y:
            x = _out_proj("lru", (h_f, h_b), proj, 1, None, w_out16, x, gate, fg, col_major=cm)
        outs.append(x)
    return outs


def kernel(x, c, ctx, c_ctx, mod_w, mod_b, norm_g, gdn_w_in, gdn_conv, gdn_a_log, gdn_dt_bias, gdn_norm_g,
           gdn_w_out, lru_w_in, lru_conv_w, lru_conv_b, lru_w_r, lru_b_r, lru_w_i, lru_b_i, lru_lambda,
           lru_w_out, final_g):
    b, _, d = x.shape
    depth = mod_w.shape[0]
    assert b + 1 <= MOD_ROWS
    cond = jnp.concatenate([c, c_ctx[None], jnp.zeros((MOD_ROWS - b - 1, d), F32)], axis=0)
    mod = _modulation(cond, mod_w, mod_b)
    for i in range(depth):
        lat = [mod[i, :b, k * d:(k + 1) * d].reshape(b, 1, d) for k in range(3)]
        cmod = [jnp.broadcast_to(mod[i, b, k * d:(k + 1) * d].reshape(1, 1, d), (b, 1, d)) for k in range(3)]
        col_major = (i + i // 2) % 2 == 1
        last = i == depth - 1
        j = i // 2
        common = dict(col_major=col_major, final_g=final_g if last else None, update_ctx=not last)
        if i % 2 == 0:
            ctx, x = _gdn_layer((ctx, x), (cmod, lat), norm_g[i], gdn_w_in[j], gdn_conv[j], gdn_a_log[j],
                                gdn_dt_bias[j], gdn_norm_g[j], gdn_w_out[j], **common)
        else:
            ctx, x = _lru_layer((ctx, x), (cmod, lat), norm_g[i], lru_w_in[j], lru_conv_w[j], lru_conv_b[j],
                                lru_w_r[j], lru_b_r[j], lru_w_i[j], lru_b_i[j], lru_lambda[j], lru_w_out[j],
                                **common)
    return x
```

```python
import functools

import numpy as np
import jax
import jax.numpy as jnp
from jax import lax
from jax.experimental import pallas as pl
from jax.experimental.pallas import tpu as pltpu

F32 = jnp.float32
BF16 = jnp.bfloat16

EPS = 1e-6
GRID_W = 64
HEAD_DIM = 128
QK_HEADS = 8
V_HEADS = 16
CHUNK = 64
CONV_W = 4
CONV_LEFT = 2
HALO = 8
HALO_BLK = 16
LRU_BLOCKS = 4
RG_C = 8.0
LANES = 128
SUB = 8
MOD_ROWS = 8

VMEM_LIMIT = 56 * 1024 * 1024


def _cparams(sem):
    return pltpu.CompilerParams(dimension_semantics=sem, vmem_limit_bytes=VMEM_LIMIT)


def _pick(total, target, mult):
    best = None
    for cand in range(mult, min(total, target) + 1, mult):
        if total % cand == 0:
            best = cand
    assert best is not None, (total, target, mult)
    return best


def _sigmoid(x):
    return 1.0 / (1.0 + jnp.exp(-x))


def _softplus(x):
    return jnp.maximum(x, 0.0) + jnp.log(1.0 + jnp.exp(-jnp.abs(x)))


def _dot(a, b):
    return jnp.dot(a, b, preferred_element_type=F32)


def _dot_nt(a, b):
    return lax.dot_general(a, b, (((1,), (1,)), ((), ())), preferred_element_type=F32)


def _dot_tn(a, b):
    return lax.dot_general(a, b, (((0,), (0,)), ((), ())), preferred_element_type=F32)


def _mod_kernel(c_ref, w_ref, b_ref, o_ref):
    c = c_ref[...]
    sc = c * _sigmoid(c)
    o_ref[...] = jnp.dot(sc, w_ref[...], preferred_element_type=F32,
                         precision=lax.Precision.HIGHEST) + b_ref[...]


def _modulation(cond, mod_w, mod_b):
    depth, d, n3 = mod_w.shape
    return pl.pallas_call(
        _mod_kernel,
        grid=(depth, n3 // d),
        in_specs=[pl.BlockSpec((MOD_ROWS, d), lambda i, j: (0, 0)),
                  pl.BlockSpec((None, d, d), lambda i, j: (i, 0, j)),
                  pl.BlockSpec((None, 1, d), lambda i, j: (i, 0, j))],
        out_specs=pl.BlockSpec((None, MOD_ROWS, d), lambda i, j: (i, 0, j)),
        out_shape=jax.ShapeDtypeStruct((depth, MOD_ROWS, n3), F32),
        compiler_params=_cparams(("parallel", "parallel")),
        name="modulation",
    )(cond, mod_w, mod_b.reshape(depth, 1, n3))


def _stream_tiling(t, d, col_major, target):
    if not col_major:
        tm = _pick(t, target, CHUNK)
        return tm, 1, (lambda a: a), (None, tm, d), (lambda b, i: (b, i, 0))
    rows = t // GRID_W
    assert rows * GRID_W == t and rows % 8 == 0
    nw = _pick(GRID_W, max(target // rows, 1), 1)
    tm = rows * nw
    return (tm, nw, (lambda a: a.reshape(a.shape[0], rows, GRID_W * d)),
            (None, rows, nw * d), (lambda b, i: (b, 0, i)))


def _load_stream_tile(x_ref, nw, d):
    if nw == 1:
        return x_ref[...]
    return jnp.concatenate([x_ref[:, k * d:(k + 1) * d] for k in range(nw)], axis=0)


def _in_proj_kernel(*refs, nw, d, has_extra):
    if has_extra:
        x_ref, g_ref, sh_ref, sc_ref, w_ref, we_ref, o_ref, oe_ref, h_scr = refs
    else:
        x_ref, g_ref, sh_ref, sc_ref, w_ref, o_ref, h_scr = refs

    @pl.when(pl.program_id(2) == 0)
    def _():
        x = _load_stream_tile(x_ref, nw, d)
        y = x * lax.rsqrt(jnp.mean(x * x, axis=-1, keepdims=True) + EPS) * g_ref[...]
        h = (y * (1.0 + sc_ref[...]) + sh_ref[...]).astype(BF16)
        h_scr[...] = h
        if has_extra:
            oe_ref[...] = _dot(h, we_ref[...])

    o_ref[...] = _dot(h_scr[...], w_ref[...]).astype(BF16)


def _in_proj(x, norm_g, shift, scale, w, w_extra, *, col_major):
    b, t, d = x.shape
    n = w.shape[1]
    tn = _pick(n, 1024, LANES)
    tm, nw, view, xblock, xmap = _stream_tiling(t, d, col_major, 1024)
    has_extra = w_extra is not None
    vec = pl.BlockSpec((None, 1, d), lambda bi, i, j: (bi, 0, 0))
    in_specs = [pl.BlockSpec(xblock, lambda bi, i, j: xmap(bi, i)),
                pl.BlockSpec((1, d), lambda bi, i, j: (0, 0)), vec, vec,
                pl.BlockSpec((d, tn), lambda bi, i, j: (0, j))]
    out_specs = [pl.BlockSpec((None, tm, tn), lambda bi, i, j: (bi, i, j))]
    out_shape = [jax.ShapeDtypeStruct((b, t, n), BF16)]
    args = [view(x), norm_g.reshape(1, d), shift, scale, w]
    if has_extra:
        in_specs.append(pl.BlockSpec((d, LANES), lambda bi, i, j: (0, 0)))
        out_specs.append(pl.BlockSpec((None, tm, LANES), lambda bi, i, j: (bi, i, 0)))
        out_shape.append(jax.ShapeDtypeStruct((b, t, LANES), F32))
        args.append(w_extra)
    outs = pl.pallas_call(
        functools.partial(_in_proj_kernel, nw=nw, d=d, has_extra=has_extra),
        grid=(b, t // tm, n // tn),
        in_specs=in_specs, out_specs=out_specs, out_shape=out_shape,
        scratch_shapes=[pltpu.VMEM((tm, d), BF16)],
        compiler_params=_cparams(("parallel", "parallel", "arbitrary")),
        name="in_proj",
    )(*args)
    return outs if has_extra else outs[0]


def _halo_specs(tm, t, width, col_block, tile_of):
    per = tm // HALO_BLK
    last = t // HALO_BLK - 1

    def cur(*g):
        bi, i = tile_of(*g)
        return (bi, i, col_block)

    def prev(*g):
        bi, i = tile_of(*g)
        return (bi, jnp.maximum(i * per - 1, 0), col_block)

    def nxt(*g):
        bi, i = tile_of(*g)
        return (bi, jnp.minimum((i + 1) * per, last), col_block)

    return [pl.BlockSpec((None, tm, width), cur),
            pl.BlockSpec((None, HALO_BLK, width), prev),
            pl.BlockSpec((None, HALO_BLK, width), nxt)]


def _fill_ext(ext, cur_ref, prev_ref, next_ref, i, nt, tm):
    ext[0:HALO, :] = jnp.where(i > 0, prev_ref[HALO_BLK - HALO:, :].astype(F32), 0.0)
    ext[HALO:HALO + tm, :] = cur_ref[...].astype(F32)
    ext[HALO + tm:, :] = jnp.where(i < nt - 1, next_ref[0:HALO, :].astype(F32), 0.0)


def _conv_taps(ext, w_ref, lanes, tm):
    acc = None
    for j in range(CONV_W):
        r0 = HALO - CONV_LEFT + j
        term = ext[r0:r0 + tm, lanes] * w_ref[j:j + 1, lanes]
        acc = term if acc is None else acc + term
    return acc


GDN_HPB = 2


XHALO = 16
GATE_GROUP = 8


def _gate_lane_permutation():
    rep = V_HEADS // QK_HEADS
    perm = np.zeros((LANES, QK_HEADS // GDN_HPB * LANES), np.float32)
    for dr in range(2):
        for kind in range(2):
            for head in range(V_HEADS):
                hg, vh = divmod(head, rep)
                grp, hl = divmod(hg, GDN_HPB)
                perm[dr * 2 * V_HEADS + kind * V_HEADS + head,
                     grp * LANES + hl * GATE_GROUP + dr * 2 * rep + kind * rep + vh] = 1.0
    return perm


def _gdn_in_kernel(x_ref, xp_ref, xn_ref, g_ref, sh_ref, sc_ref, w_ref, wz_ref, wab_ref, cw_ref, alog_ref, dtb_ref,
                   perm_ref, q_ref, k_ref, kt_ref, v_ref, z_ref, cols_ref, rows_ref, h_scr, r_scr, *y_scrs, tm, nw, d):
    i = pl.program_id(1)
    nt = pl.num_programs(1)
    j = pl.program_id(2)
    qk_w = QK_HEADS * HEAD_DIM
    tn = w_ref.shape[1]
    heads_per_tile = tn // HEAD_DIM

    def modulated(x):
        y = x * lax.rsqrt(jnp.mean(x * x, axis=-1, keepdims=True) + EPS) * g_ref[...]
        return (y * (1.0 + sc_ref[...]) + sh_ref[...]).astype(BF16)

    @pl.when(j == 0)
    def _():
        h = modulated(_load_stream_tile(x_ref, nw, d))
        h_scr[0:XHALO, :] = modulated(xp_ref[...])
        h_scr[XHALO:XHALO + tm, :] = h
        h_scr[XHALO + tm:, :] = modulated(xn_ref[...])
        ab = _dot(h, wab_ref[...])
        g = -jnp.exp(alog_ref[...]) * _softplus(ab + dtb_ref[...])
        beta = _sigmoid(ab)
        ri = lax.broadcasted_iota(jnp.int32, (CHUNK, CHUNK), 0)
        ci = lax.broadcasted_iota(jnp.int32, (CHUNK, CHUNK), 1)
        tril = (ri >= ci).astype(F32)
        triu = (ri <= ci).astype(F32)
        group = lax.broadcasted_iota(jnp.int32, (CHUNK, LANES), 1) // V_HEADS
        gates = []
        for c in range(tm // CHUNK):
            rs = slice(c * CHUNK, (c + 1) * CHUNK)
            pre = jnp.dot(tril, g[rs], preferred_element_type=F32, precision=lax.Precision.HIGHEST)
            suf = jnp.dot(triu, g[rs], preferred_element_type=F32, precision=lax.Precision.HIGHEST)
            gates.append(jnp.where(group == 0, pre, jnp.where(group == 2, suf, beta[rs])))
        cols = jnp.dot(jnp.concatenate(gates, axis=0), perm_ref[...], preferred_element_type=F32,
                       precision=lax.Precision.HIGHEST)
        cols_ref[...] = cols
        for c in range(tm // CHUNK):
            for grp in range(rows_ref.shape[0]):
                tile = cols[c * CHUNK:(c + 1) * CHUNK, grp * LANES:(grp + 1) * LANES]
                rows_ref[grp, c] = tile.T[0:GDN_HPB * GATE_GROUP, :]

    ngroup = len(y_scrs)
    gw = tn // ngroup

    def project(grp):
        y = _dot(h_scr[...], w_ref[:, grp * gw:(grp + 1) * gw])
        y_scrs[grp][0:XHALO, :] = jnp.where(i > 0, y[0:XHALO], 0.0)
        y_scrs[grp][XHALO:XHALO + tm, :] = y[XHALO:XHALO + tm]
        y_scrs[grp][XHALO + tm:, :] = jnp.where(i < nt - 1, y[XHALO + tm:], 0.0)

    def conv_silu(grp):
        for h in range(gw // HEAD_DIM):
            lanes = slice(grp * gw + h * HEAD_DIM, grp * gw + (h + 1) * HEAD_DIM)
            acc = None
            for tap in range(CONV_W):
                r0 = XHALO - CONV_LEFT + tap
                term = y_scrs[grp][r0:r0 + tm, h * HEAD_DIM:(h + 1) * HEAD_DIM] * cw_ref[tap:tap + 1, lanes]
                acc = term if acc is None else acc + term
            r_scr[:, lanes] = acc * _sigmoid(acc)

    project(0)
    for grp in range(1, ngroup):
        project(grp)
        conv_silu(grp - 1)
    z_ref[...] = _dot(h_scr[XHALO:XHALO + tm, :], wz_ref[...]).astype(BF16)
    conv_silu(ngroup - 1)

    def l2(yh):
        return yh * lax.rsqrt(jnp.sum(yh * yh, axis=-1, keepdims=True) + EPS)

    for jj in range(qk_w // tn):
        @pl.when(j == jj)
        def _(jj=jj):
            for h in range(heads_per_tile):
                lanes = slice(h * HEAD_DIM, (h + 1) * HEAD_DIM)
                q_ref[:, jj * tn + h * HEAD_DIM:jj * tn + (h + 1) * HEAD_DIM] = (
                    l2(r_scr[:, lanes]) * (HEAD_DIM ** -0.5)).astype(BF16)

    for jj in range(qk_w // tn):
        @pl.when(j == qk_w // tn + jj)
        def _(jj=jj):
            for h in range(heads_per_tile):
                kn = l2(r_scr[:, h * HEAD_DIM:(h + 1) * HEAD_DIM])
                head = jj * heads_per_tile + h
                k_ref[:, head * HEAD_DIM:(head + 1) * HEAD_DIM] = kn.astype(BF16)
                for c in range(tm // CHUNK):
                    kc = kn[c * CHUNK:(c + 1) * CHUNK]
                    kt_ref[head, c] = jnp.concatenate([kc, kc], axis=0).T.astype(BF16)

    for jj in range(V_HEADS * HEAD_DIM // tn):
        @pl.when(j == 2 * qk_w // tn + jj)
        def _(jj=jj):
            v_ref[:, jj * tn:(jj + 1) * tn] = r_scr[...].astype(BF16)


def _gdn_in(x, norm_g, shift, scale, w_main, w_ab, conv_w, a_log, dt_bias, *, col_major):
    b, t, d = x.shape
    qk_w = QK_HEADS * HEAD_DIM
    v_w = V_HEADS * HEAD_DIM
    tn = 1024
    assert w_main.shape[1] == 2 * qk_w + 2 * v_w and qk_w % tn == 0 and v_w % tn == 0
    nj = (2 * qk_w + v_w) // tn
    tz = v_w // nj
    w_qkv, w_z = w_main[:, :2 * qk_w + v_w], w_main[:, 2 * qk_w + v_w:]
    tm, nw, view, xblock, xmap = _stream_tiling(t, d, col_major, 512)
    nt = t // tm
    ngrp = QK_HEADS // GDN_HPB
    if col_major:
        rows = t // GRID_W
        assert rows % XHALO == 0
        prev = lambda bi, i, j: (bi, rows // XHALO - 1, jnp.maximum(i * nw - 1, 0))
        nxt = lambda bi, i, j: (bi, 0, jnp.minimum((i + 1) * nw, GRID_W - 1))
    else:
        per = tm // XHALO
        prev = lambda bi, i, j: (bi, jnp.maximum(i * per - 1, 0), 0)
        nxt = lambda bi, i, j: (bi, jnp.minimum((i + 1) * per, t // XHALO - 1), 0)
    zeros = jnp.zeros((2, V_HEADS), F32)
    lane_vec = lambda p: jnp.pad(jnp.stack([p, zeros], axis=1).reshape(1, 4 * V_HEADS),
                                 ((0, 0), (0, LANES - 4 * V_HEADS)))
    vec = pl.BlockSpec((None, 1, d), lambda bi, i, j: (bi, 0, 0))
    const = lambda shape: pl.BlockSpec(shape, lambda bi, i, j: (0,) * len(shape))
    tile = lambda width: pl.BlockSpec((None, tm, width), lambda bi, i, j: (bi, i, 0))
    perm = jnp.asarray(_gate_lane_permutation())
    xv = view(x)
    return pl.pallas_call(
        functools.partial(_gdn_in_kernel, tm=tm, nw=nw, d=d),
        grid=(b, nt, nj),
        in_specs=[pl.BlockSpec(xblock, lambda bi, i, j: xmap(bi, i)),
                  pl.BlockSpec((None, XHALO, d), prev), pl.BlockSpec((None, XHALO, d), nxt),
                  const((1, d)), vec, vec,
                  pl.BlockSpec((d, tn), lambda bi, i, j: (0, j)),
                  pl.BlockSpec((d, tz), lambda bi, i, j: (0, j)),
                  const((d, LANES)), pl.BlockSpec((CONV_W, tn), lambda bi, i, j: (0, j)),
                  const((1, LANES)), const((1, LANES)), const(perm.shape)],
        out_specs=[tile(qk_w), tile(qk_w),
                   pl.BlockSpec((None, QK_HEADS, tm // CHUNK, HEAD_DIM, 2 * CHUNK), lambda bi, i, j: (bi, 0, i, 0, 0)),
                   tile(v_w), pl.BlockSpec((None, tm, tz), lambda bi, i, j: (bi, i, j)), tile(ngrp * LANES),
                   pl.BlockSpec((None, ngrp, tm // CHUNK, GDN_HPB * GATE_GROUP, CHUNK),
                                lambda bi, i, j: (bi, 0, i, 0, 0))],
        out_shape=[jax.ShapeDtypeStruct((b, t, qk_w), BF16), jax.ShapeDtypeStruct((b, t, qk_w), BF16),
                   jax.ShapeDtypeStruct((b, QK_HEADS, t // CHUNK, HEAD_DIM, 2 * CHUNK), BF16),
                   jax.ShapeDtypeStruct((b, t, v_w), BF16), jax.ShapeDtypeStruct((b, t, v_w), BF16),
                   jax.ShapeDtypeStruct((b, t, ngrp * LANES), F32),
                   jax.ShapeDtypeStruct((b, ngrp, t // CHUNK, GDN_HPB * GATE_GROUP, CHUNK), F32)],
        scratch_shapes=([pltpu.VMEM((tm + 2 * XHALO, d), BF16), pltpu.VMEM((tm, tn), F32)]
                        + [pltpu.VMEM((tm + 2 * XHALO, 2 * HEAD_DIM), F32)] * (tn // (2 * HEAD_DIM))),
        compiler_params=_cparams(("parallel", "parallel", "arbitrary")),
        name="gdn_in",
    )(xv, xv, xv, norm_g.reshape(1, d), shift, scale, w_qkv, w_z, w_ab, conv_w, lane_vec(a_log), lane_vec(dt_bias),
      perm)


INV_BASE = 8


def _block_diag2(pair, first):
    return jnp.concatenate([jnp.where(first, pair, 0.0), jnp.where(first, 0.0, pair)], axis=0).astype(BF16)


SCAN_STAGES = 13
SCAN_SKEW = 2
SCAN_CHUNKS = 16


def _gdn_scan_kernel(qf_ref, kf_ref, tf_ref, vf_ref, cf_ref, rf_ref, qb_ref, kb_ref, tb_ref, vb_ref, cb_ref,
                     rb_ref, s0_ref, of_ref, ob_ref, sout_ref, s_scr, *, cpb, hpb):
    n = pl.program_id(2)

    @pl.when(n == 0)
    def _():
        s_scr[...] = s0_ref[...]

    ri = lax.broadcasted_iota(jnp.int32, (CHUNK, 2 * CHUNK), 0)
    lane = lax.broadcasted_iota(jnp.int32, (CHUNK, 2 * CHUNK), 1)
    first = lane < CHUNK
    first_row = first[0:1]
    ci = jnp.where(first, lane, lane - CHUNK)
    eye = (ri == ci).astype(F32)
    incl = (ri >= ci, ri <= ci)
    strict = (ri > ci, ri < ci)
    same_block = lambda size: (ri // size) == (ci // size)
    refs = ((qf_ref, kf_ref, tf_ref, vf_ref, cf_ref, rf_ref, of_ref),
            (qb_ref, kb_ref, tb_ref, vb_ref, cb_ref, rb_ref, ob_ref))
    pairs = [(hl, d) for hl in range(hpb) for d in range(2)]
    hd = HEAD_DIM
    bd = lambda xs: [_block_diag2(x, first) for x in xs]
    zero_s = jnp.zeros((hd, hd), BF16)
    zero_v = jnp.zeros((CHUNK, hd), BF16)
    state = [s_scr[hl, d] for hl, d in pairs]

    def chunk_stages(c):
        probs = []
        for hl, d in pairs:
            cc = c if d == 0 else cpb - 1 - c
            probs.append((hl, d, cc, slice(cc * CHUNK, (cc + 1) * CHUNK)))
        idx = range(len(probs))
        gram = [_dot(jnp.concatenate([refs[d][0][sl, hl * hd:(hl + 1) * hd],
                                      refs[d][1][sl, hl * hd:(hl + 1) * hd]], axis=0), refs[d][2][hl, cc])
                for hl, d, cc, sl in probs]
        yield
        lmat, lhs2, qd, scale_u, scale_w, gl = [], [], [], [], [], []
        for j, (hl, d, cc, sl) in enumerate(probs):
            q_ref, k_ref, t_ref, v_ref, c_ref, r_ref, _ = refs[d]
            cols = c_ref[sl, :]
            rows = r_ref[cc]
            g0 = hl * GATE_GROUP + d * 4
            pair_row = lambda r: jnp.concatenate([rows[r:r + 1, :], rows[r + 1:r + 2, :]], axis=1)
            grow = pair_row(g0)
            brow = pair_row(g0 + 2)
            gcols = [cols[:, g0 + vh:g0 + vh + 1] for vh in range(2)]
            bcols = [cols[:, g0 + 2 + vh:g0 + 3 + vh] for vh in range(2)]
            gtots = [g[CHUNK - 1:CHUNK, :] if d == 0 else g[0:1, :] for g in gcols]
            gpair = jnp.where(first, gcols[0], gcols[1])
            bpair = jnp.where(first, bcols[0], bcols[1])
            decay = jnp.where(incl[d], jnp.exp(jnp.where(incl[d], gpair - grow, 0.0)), 0.0)
            lmat.append(jnp.where(strict[d], gram[j][CHUNK:] * bpair * decay, 0.0))
            qkm = gram[j][:CHUNK] * decay
            kdt = t_ref[hl, cc].astype(F32) * jnp.exp(jnp.where(first_row, gtots[0], gtots[1]) - grow)
            lhs2.append(jnp.concatenate([qkm, kdt], axis=0).astype(BF16))
            q = q_ref[sl, hl * hd:(hl + 1) * hd].astype(F32)
            qd.append(jnp.concatenate([q * jnp.exp(g) for g in gcols], axis=1))
            scale_u.append(brow)
            scale_w.append(brow * jnp.exp(grow))
            gl.append(jnp.concatenate([jnp.broadcast_to(jnp.exp(g), (1, hd)) for g in gtots], axis=1))
        base = same_block(INV_BASE)
        p = [-jnp.where(base, l, 0.0) for l in lmat]
        p2 = [_dot(p[i].astype(BF16), m) for i, m in enumerate(bd(p))]
        yield
        p2_bd = bd(p2)
        p4 = [_dot(p2[i].astype(BF16), p2_bd[i]) for i in idx]
        tinv = [eye + x for x in p]
        tinv = [tinv[i] + _dot(tinv[i].astype(BF16), p2_bd[i]) for i in idx]
        yield
        tinv = [tinv[i] + _dot(tinv[i].astype(BF16), m) for i, m in enumerate(bd(p4))]
        yield
        size = INV_BASE
        while size < CHUNK:
            off_diag = same_block(2 * size) & jnp.logical_not(same_block(size))
            a16 = [jnp.where(off_diag, l, 0.0).astype(BF16) for l in lmat]
            at = [_dot(a16[i], m) for i, m in enumerate(bd(tinv))]
            yield
            tinv = [tinv[i] - _dot(tinv[i].astype(BF16), m) for i, m in enumerate(bd(at))]
            yield
            size *= 2
        u, w = [], []
        for j, (hl, d, cc, sl) in enumerate(probs):
            k16 = refs[d][1][sl, hl * hd:(hl + 1) * hd]
            v16 = [refs[d][3][sl, (hl * 2 + vh) * hd:(hl * 2 + vh + 1) * hd] for vh in range(2)]
            v_bd16 = jnp.concatenate([jnp.concatenate([v16[0], zero_v], axis=1),
                                      jnp.concatenate([zero_v, v16[1]], axis=1)], axis=0)
            k_bd16 = jnp.concatenate([jnp.concatenate([k16, zero_v], axis=1),
                                      jnp.concatenate([zero_v, k16], axis=1)], axis=0)
            u.append(_dot((tinv[j] * scale_u[j]).astype(BF16), v_bd16))
            w.append(_dot((tinv[j] * scale_w[j]).astype(BF16), k_bd16))
        yield
        wq16 = [jnp.concatenate([w[j], qd[j]], axis=0).astype(BF16) for j in idx]
        s16 = [x.astype(BF16) for x in state]
        s_bd = [jnp.concatenate([jnp.concatenate([x[:, :hd], zero_s], axis=1),
                                 jnp.concatenate([zero_s, x[:, hd:]], axis=1)], axis=0) for x in s16]
        ws = [_dot(wq16[i], s_bd[i]) for i in idx]
        yield
        vnew16 = [(u[i] - ws[i][:CHUNK]).astype(BF16) for i in idx]
        v_bd = [jnp.concatenate([jnp.concatenate([x[:, :hd], zero_v], axis=1),
                                 jnp.concatenate([zero_v, x[:, hd:]], axis=1)], axis=0) for x in vnew16]
        upd = [_dot(lhs2[i], v_bd[i]) for i in idx]
        for i, (hl, d, cc, sl) in enumerate(probs):
            state[i] = state[i] * gl[i] + upd[i][CHUNK:]
            refs[d][6][sl, hl * 2 * hd:(hl + 1) * 2 * hd] = ws[i][CHUNK:] + upd[i][:CHUNK]

    gens = [chunk_stages(c) for c in range(cpb)]
    for tick in range(SCAN_STAGES + SCAN_SKEW * (cpb - 1)):
        for c in range(cpb):
            if 0 <= tick - SCAN_SKEW * c < SCAN_STAGES:
                next(gens[c], None)
    for i, (hl, d) in enumerate(pairs):
        s_scr[hl, d] = state[i]

    @pl.when(n == pl.num_programs(2) - 1)
    def _():
        sout_ref[...] = s_scr[...]


def _gdn_scan(qn, kn, kt, v, cols, rows, s0):
    b, t, _ = qn.shape
    rep = V_HEADS // QK_HEADS
    assert rep == 2
    hpb = GDN_HPB
    blk = _pick(t, SCAN_CHUNKS * CHUNK, CHUNK)
    cpb = blk // CHUNK
    nb = t // blk
    fwd = lambda n: n
    bwd = lambda n: nb - 1 - n

    def specs(order):
        return [pl.BlockSpec((None, blk, hpb * HEAD_DIM), lambda bi, h, n: (bi, order(n), h)),
                pl.BlockSpec((None, blk, hpb * HEAD_DIM), lambda bi, h, n: (bi, order(n), h)),
                pl.BlockSpec((None, hpb, cpb, HEAD_DIM, rep * CHUNK), lambda bi, h, n: (bi, h, order(n), 0, 0)),
                pl.BlockSpec((None, blk, hpb * rep * HEAD_DIM), lambda bi, h, n: (bi, order(n), h)),
                pl.BlockSpec((None, blk, LANES), lambda bi, h, n: (bi, order(n), h)),
                pl.BlockSpec((None, None, cpb, hpb * GATE_GROUP, CHUNK), lambda bi, h, n: (bi, h, order(n), 0, 0))]

    state_block = (None, hpb, 2, HEAD_DIM, rep * HEAD_DIM)
    state_spec = pl.BlockSpec(state_block, lambda bi, h, n: (bi, h, 0, 0, 0))
    o_shape = jax.ShapeDtypeStruct((b, t, V_HEADS * HEAD_DIM), F32)
    return pl.pallas_call(
        functools.partial(_gdn_scan_kernel, cpb=cpb, hpb=hpb),
        grid=(b, QK_HEADS // hpb, nb),
        in_specs=specs(fwd) + specs(bwd) + [state_spec],
        out_specs=[pl.BlockSpec((None, blk, hpb * rep * HEAD_DIM), lambda bi, h, n: (bi, fwd(n), h)),
                   pl.BlockSpec((None, blk, hpb * rep * HEAD_DIM), lambda bi, h, n: (bi, bwd(n), h)),
                   state_spec],
        out_shape=[o_shape, o_shape, jax.ShapeDtypeStruct(s0.shape, F32)],
        scratch_shapes=[pltpu.VMEM(state_block[1:], F32)],
        compiler_params=_cparams(("parallel", "parallel", "arbitrary")),
        name="gdn_scan",
    )(qn, kn, kt, v, cols, rows, qn, kn, kt, v, cols, rows, s0)


def _lru_scan_kernel(fc_ref, fp_ref, fn_ref, bc_ref, bp_ref, bn_ref, cw_ref, cb_ref, wr_ref, br_ref,
                     wi_ref, bi_ref, lam_ref, h0_ref, hf_ref, hb_ref, hfin_ref, ext, carry, *, tm):
    i = pl.program_id(1)
    nt = pl.num_programs(1)

    @pl.when(i == 0)
    def _():
        carry[...] = h0_ref[...]

    width = cw_ref.shape[1]
    bs = width // LRU_BLOCKS
    nv = tm // SUB
    tpb = bs // LANES
    sub = lax.broadcasted_iota(jnp.int32, (SUB, bs), 0)
    dirs = ((fc_ref, fp_ref, fn_ref, hf_ref, i), (bc_ref, bp_ref, bn_ref, hb_ref, nt - 1 - i))
    for d, (c_ref, p_ref, n_ref, o_ref, tile) in enumerate(dirs):
        for lt in range(width // LANES):
            lt_lanes = slice(lt * LANES, (lt + 1) * LANES)
            ext[lt, 0:HALO, :] = jnp.where(tile > 0, p_ref[HALO_BLK - HALO:, lt_lanes].astype(F32), 0.0)
            ext[lt, HALO:HALO + tm, :] = c_ref[:, lt_lanes].astype(F32)
            ext[lt, HALO + tm:, :] = jnp.where(tile < nt - 1, n_ref[0:HALO, lt_lanes].astype(F32), 0.0)
        for blk in range(LRU_BLOCKS):
            lanes = slice(blk * bs, (blk + 1) * bs)
            groups = []
            for v in range(nv):
                pieces = []
                for lt in range(blk * tpb, (blk + 1) * tpb):
                    acc = None
                    for j in range(CONV_W):
                        term = (ext[lt, pl.ds(HALO - CONV_LEFT + j + v, SUB, stride=nv), :]
                                * cw_ref[j:j + 1, lt * LANES:(lt + 1) * LANES])
                        acc = term if acc is None else acc + term
                    pieces.append(acc)
                groups.append(jnp.concatenate(pieces, axis=1))
            xc = jnp.concatenate(groups, axis=0) + cb_ref[:, lanes]
            xc16 = xc.astype(BF16)
            r = _sigmoid(_dot(xc16, wr_ref[d, blk]) + br_ref[d:d + 1, lanes])
            gi = _sigmoid(_dot(xc16, wi_ref[d, blk]) + bi_ref[d:d + 1, lanes])
            log_a = -RG_C * r * _softplus(-lam_ref[d:d + 1, lanes])
            a = jnp.exp(log_a)
            bv = jnp.sqrt(1.0 - jnp.exp(2.0 * log_a)) * (gi * xc)
            order = range(nv) if d == 0 else range(nv - 1, -1, -1)
            hs, acums = [None] * nv, [None] * nv
            h_run = a_run = None
            for v in order:
                av, bvv = a[v * SUB:(v + 1) * SUB], bv[v * SUB:(v + 1) * SUB]
                h_run = bvv if h_run is None else av * h_run + bvv
                a_run = av if a_run is None else av * a_run
                hs[v], acums[v] = h_run, a_run
            e_end, p_end = h_run, a_run
            s = 1
            while s < SUB:
                shift = s if d == 0 else SUB - s
                keep = (sub >= s) if d == 0 else (sub < SUB - s)
                p_prev = jnp.where(keep, pltpu.roll(p_end, shift, 0), 1.0)
                e_prev = jnp.where(keep, pltpu.roll(e_end, shift, 0), 0.0)
                e_end = p_end * e_prev + e_end
                p_end = p_end * p_prev
                s *= 2
            cin = jnp.broadcast_to(carry[d, 0:1, lanes], (SUB, bs))
            one = 1 if d == 0 else SUB - 1
            first_sub = (sub == 0) if d == 0 else (sub == SUB - 1)
            start = jnp.where(first_sub, cin, pltpu.roll(p_end, one, 0) * cin + pltpu.roll(e_end, one, 0))
            for v in range(nv):
                h = acums[v] * start + hs[v]
                for k in range(tpb):
                    o_ref[blk * tpb + k, pl.ds(v, SUB, stride=nv), :] = h[:, k * LANES:(k + 1) * LANES]
            last = SUB - 1 if d == 0 else 0
            carry[d, 0:1, lanes] = (p_end * cin + e_end)[last:last + 1, :]

    @pl.when(i == nt - 1)
    def _():
        hfin_ref[...] = carry[...]


def _lru_scan(proj, conv_w, conv_b, w_r, b_r, w_i, b_i, lam, h0):
    b, t, two_w = proj.shape
    width = two_w // 2
    tm = _pick(t, 256, CHUNK)
    nt = t // tm
    nlt = width // LANES
    whole = lambda a: pl.BlockSpec(a.shape, lambda bi, i: (0,) * a.ndim)
    state_spec = pl.BlockSpec((None, 2, 1, width), lambda bi, i: (bi, 0, 0, 0))
    consts = [conv_w, conv_b.reshape(1, width), w_r, b_r, w_i, b_i, lam]
    h_shape = jax.ShapeDtypeStruct((b, nlt, t, LANES), F32)
    return pl.pallas_call(
        functools.partial(_lru_scan_kernel, tm=tm),
        grid=(b, nt),
        in_specs=(_halo_specs(tm, t, width, 0, lambda bi, i: (bi, i))
                  + _halo_specs(tm, t, width, 0, lambda bi, i: (bi, nt - 1 - i))
                  + [whole(a) for a in consts] + [state_spec]),
        out_specs=[pl.BlockSpec((None, nlt, tm, LANES), lambda bi, i: (bi, 0, i, 0)),
                   pl.BlockSpec((None, nlt, tm, LANES), lambda bi, i: (bi, 0, nt - 1 - i, 0)),
                   state_spec],
        out_shape=[h_shape, h_shape, jax.ShapeDtypeStruct(h0.shape, F32)],
        scratch_shapes=[pltpu.VMEM((nlt, tm + 2 * HALO, LANES), F32),
                        pltpu.VMEM((2, 1, width), F32)],
        compiler_params=_cparams(("parallel", "arbitrary")),
        name="lru_scan",
    )(proj, proj, proj, proj, proj, proj, *consts, h0)


def _residual_store(xo_ref, x_ref, y, gate, fg_ref, nw, d, rows):
    for k in range(nw):
        lanes = slice(k * d, (k + 1) * d) if nw > 1 else slice(None)
        yk = y[k * rows:(k + 1) * rows] if nw > 1 else y
        xn = x_ref[:, lanes] + gate * yk
        if fg_ref is not None:
            xn = xn * lax.rsqrt(jnp.mean(xn * xn, axis=-1, keepdims=True) + EPS) * fg_ref[...]
        xo_ref[:, lanes] = xn


def _gdn_out_kernel(*refs, nw, d, rows, final):
    of_ref, ob_ref, z_ref, ng_ref, w_ref, x_ref, gate_ref = refs[:7]
    fg_ref = refs[7] if final else None
    xo_ref = refs[-1]
    parts = []
    for h in range(V_HEADS):
        lanes = slice(h * HEAD_DIM, (h + 1) * HEAD_DIM)
        o = of_ref[:, lanes] + ob_ref[:, lanes]
        o = o * lax.rsqrt(jnp.mean(o * o, axis=-1, keepdims=True) + EPS) * ng_ref[...]
        z = z_ref[:, lanes].astype(F32)
        parts.append((o * (z * _sigmoid(z))).astype(BF16))
    y = _dot(jnp.concatenate(parts, axis=1), w_ref[...])
    _residual_store(xo_ref, x_ref, y, gate_ref[...], fg_ref, nw, d, rows)


def _lru_out_kernel(*refs, nw, d, rows, final):
    hf_ref, hb_ref, gt_ref, w_ref, x_ref, gate_ref = refs[:6]
    fg_ref = refs[6] if final else None
    xo_ref = refs[-1]
    gt = gt_ref[...].astype(F32)
    h = jnp.concatenate([hf_ref[lt] + hb_ref[lt] for lt in range(hf_ref.shape[0])], axis=1)
    hg = (h * (gt * _sigmoid(gt))).astype(BF16)
    y = _dot(hg, w_ref[...])
    _residual_store(xo_ref, x_ref, y, gate_ref[...], fg_ref, nw, d, rows)


def _out_proj(kind, branch, gate_src, gate_col, extra, w_out, x, gate, final_g, *, col_major):
    b, t, d = x.shape
    wb = w_out.shape[0]
    tm, nw, view, xblock, xmap = _stream_tiling(t, d, col_major, 256)
    tile = lambda col: pl.BlockSpec((None, tm, wb), lambda bi, i: (bi, i, col))
    if kind == "gdn":
        in_specs = [tile(0), tile(0), tile(gate_col), pl.BlockSpec((1, HEAD_DIM), lambda bi, i: (0, 0))]
        args = [branch[0], branch[1], gate_src, extra.reshape(1, HEAD_DIM)]
        body = _gdn_out_kernel
    else:
        lane_tiles = pl.BlockSpec((None, wb // LANES, tm, LANES), lambda bi, i: (bi, 0, i, 0))
        in_specs = [lane_tiles, lane_tiles, tile(gate_col)]
        args = [branch[0], branch[1], gate_src]
        body = _lru_out_kernel
    in_specs += [pl.BlockSpec(w_out.shape, lambda bi, i: (0, 0)),
                 pl.BlockSpec(xblock, lambda bi, i: xmap(bi, i)),
                 pl.BlockSpec((None, 1, d), lambda bi, i: (bi, 0, 0))]
    args += [w_out, view(x), gate]
    final = final_g is not None
    if final:
        in_specs.append(pl.BlockSpec((1, d), lambda bi, i: (0, 0)))
        args.append(final_g.reshape(1, d))
    xv = view(x)
    out = pl.pallas_call(
        functools.partial(body, nw=nw, d=d, rows=tm // nw, final=final),
        grid=(b, t // tm),
        in_specs=in_specs,
        out_specs=pl.BlockSpec(xblock, lambda bi, i: xmap(bi, i)),
        out_shape=jax.ShapeDtypeStruct(xv.shape, F32),
        compiler_params=_cparams(("parallel", "parallel")),
        name=kind + "_out",
    )(*args)
    return out.reshape(b, t, d)


def _gdn_layer(streams, mods, norm_g, w_in, conv_w, a_log, dt_bias, head_g, w_out, *, col_major, final_g,
               update_ctx):
    qkvz_w = 2 * QK_HEADS * HEAD_DIM + 2 * V_HEADS * HEAD_DIM
    w_main = w_in[:, :qkvz_w].astype(BF16)
    w_ab = jnp.pad(w_in[:, qkvz_w:], ((0, 0), (0, LANES - (w_in.shape[1] - qkvz_w)))).astype(BF16)
    w_out16 = w_out.astype(BF16)
    state = None
    outs = []
    for (x, cm, fg, need_y), (shift, scale, gate) in zip(
            ((streams[0], False, None, update_ctx), (streams[1], col_major, final_g, True)), mods):
        b = x.shape[0]
        if state is None:
            state = jnp.zeros((b, QK_HEADS, 2, HEAD_DIM, V_HEADS // QK_HEADS * HEAD_DIM), F32)
        qn, kn, kt, v, z, cols, rows = _gdn_in(x, norm_g, shift, scale, w_main, w_ab, conv_w, a_log, dt_bias,
                                               col_major=cm)
        o_f, o_b, state = _gdn_scan(qn, kn, kt, v, cols, rows, state)
        if need_y:
            x = _out_proj("gdn", (o_f, o_b), z, 0, head_g, w_out16, x, gate, fg, col_major=cm)
        outs.append(x)
    return outs


def _lru_layer(streams, mods, norm_g, w_in, conv_w, conv_b, w_r, b_r, w_i, b_i, lam, w_out, *, col_major,
               final_g, update_ctx):
    w_in16 = w_in.astype(BF16)
    w_r16 = w_r.astype(BF16)
    w_i16 = w_i.astype(BF16)
    w_out16 = w_out.astype(BF16)
    state = None
    outs = []
    for (x, cm, fg, need_y), (shift, scale, gate) in zip(
            ((streams[0], False, None, update_ctx), (streams[1], col_major, final_g, True)), mods):
        b = x.shape[0]
        if state is None:
            state = jnp.zeros((b, 2, 1, conv_w.shape[1]), F32)
        proj = _in_proj(x, norm_g, shift, scale, w_in16, None, col_major=cm)
        h_f, h_b, state = _lru_scan(proj, conv_w, conv_b, w_r16, b_r, w_i16, b_i, lam, state)
        if need_y:
            x = _out_proj("lru", (h_f, h_b), proj, 1, None, w_out16, x, gate, fg, col_major=cm)
        outs.append(x)
    return outs


def kernel(x, c, ctx, c_ctx, mod_w, mod_b, norm_g, gdn_w_in, gdn_conv, gdn_a_log, gdn_dt_bias, gdn_norm_g,
           gdn_w_out, lru_w_in, lru_conv_w, lru_conv_b, lru_w_r, lru_b_r, lru_w_i, lru_b_i, lru_lambda,
           lru_w_out, final_g):
    b, _, d = x.shape
    depth = mod_w.shape[0]
    assert b + 1 <= MOD_ROWS
    cond = jnp.concatenate([c, c_ctx[None], jnp.zeros((MOD_ROWS - b - 1, d), F32)], axis=0)
    mod = _modulation(cond, mod_w, mod_b)
    for i in range(depth):
        lat = [mod[i, :b, k * d:(k + 1) * d].reshape(b, 1, d) for k in range(3)]
        cmod = [jnp.broadcast_to(mod[i, b, k * d:(k + 1) * d].reshape(1, 1, d), (b, 1, d)) for k in range(3)]
        col_major = (i + i // 2) % 2 == 1
        last = i == depth - 1
        j = i // 2
        common = dict(col_major=col_major, final_g=final_g if last else None, update_ctx=not last)
        if i % 2 == 0:
            ctx, x = _gdn_layer((ctx, x), (cmod, lat), norm_g[i], gdn_w_in[j], gdn_conv[j], gdn_a_log[j],
                                gdn_dt_bias[j], gdn_norm_g[j], gdn_w_out[j], **common)
        else:
            ctx, x = _lru_layer((ctx, x), (cmod, lat), norm_g[i], lru_w_in[j], lru_conv_w[j], lru_conv_b[j],
                                lru_w_r[j], lru_b_r[j], lru_w_i[j], lru_b_i[j], lru_lambda[j], lru_w_out[j],
                                **common)
    return x
```

```python
import functools

import numpy as np
import jax
import jax.numpy as jnp
from jax import lax
from jax.experimental import pallas as pl
from jax.experimental.pallas import tpu as pltpu

F32 = jnp.float32
BF16 = jnp.bfloat16

EPS = 1e-6
GRID_W = 64
HEAD_DIM = 128
QK_HEADS = 8
V_HEADS = 16
CHUNK = 64
CONV_W = 4
CONV_LEFT = 2
HALO = 8
HALO_BLK = 16
LRU_BLOCKS = 4
RG_C = 8.0
LANES = 128
SUB = 8
MOD_ROWS = 8

VMEM_LIMIT = 56 * 1024 * 1024


def _cparams(sem):
    return pltpu.CompilerParams(dimension_semantics=sem, vmem_limit_bytes=VMEM_LIMIT)


def _pick(total, target, mult):
    best = None
    for cand in range(mult, min(total, target) + 1, mult):
        if total % cand == 0:
            best = cand
    assert best is not None, (total, target, mult)
    return best


def _sigmoid(x):
    return 1.0 / (1.0 + jnp.exp(-x))


def _softplus(x):
    return jnp.maximum(x, 0.0) + jnp.log(1.0 + jnp.exp(-jnp.abs(x)))


def _dot(a, b):
    return jnp.dot(a, b, preferred_element_type=F32)


def _dot_nt(a, b):
    return lax.dot_general(a, b, (((1,), (1,)), ((), ())), preferred_element_type=F32)


def _dot_tn(a, b):
    return lax.dot_general(a, b, (((0,), (0,)), ((), ())), preferred_element_type=F32)


def _mod_kernel(c_ref, w_ref, b_ref, o_ref):
    c = c_ref[...]
    sc = c * _sigmoid(c)
    o_ref[...] = jnp.dot(sc, w_ref[...], preferred_element_type=F32,
                         precision=lax.Precision.HIGHEST) + b_ref[...]


def _modulation(cond, mod_w, mod_b):
    depth, d, n3 = mod_w.shape
    return pl.pallas_call(
        _mod_kernel,
        grid=(depth, n3 // d),
        in_specs=[pl.BlockSpec((MOD_ROWS, d), lambda i, j: (0, 0)),
                  pl.BlockSpec((None, d, d), lambda i, j: (i, 0, j)),
                  pl.BlockSpec((None, 1, d), lambda i, j: (i, 0, j))],
        out_specs=pl.BlockSpec((None, MOD_ROWS, d), lambda i, j: (i, 0, j)),
        out_shape=jax.ShapeDtypeStruct((depth, MOD_ROWS, n3), F32),
        compiler_params=_cparams(("parallel", "parallel")),
        name="modulation",
    )(cond, mod_w, mod_b.reshape(depth, 1, n3))


def _stream_tiling(t, d, col_major, target):
    if not col_major:
        tm = _pick(t, target, CHUNK)
        return tm, 1, (lambda a: a), (None, tm, d), (lambda b, i: (b, i, 0))
    rows = t // GRID_W
    assert rows * GRID_W == t and rows % 8 == 0
    nw = _pick(GRID_W, max(target // rows, 1), 1)
    tm = rows * nw
    return (tm, nw, (lambda a: a.reshape(a.shape[0], rows, GRID_W * d)),
            (None, rows, nw * d), (lambda b, i: (b, 0, i)))


def _load_stream_tile(x_ref, nw, d):
    if nw == 1:
        return x_ref[...]
    return jnp.concatenate([x_ref[:, k * d:(k + 1) * d] for k in range(nw)], axis=0)


def _in_proj_kernel(*refs, nw, d, has_extra):
    if has_extra:
        x_ref, g_ref, sh_ref, sc_ref, w_ref, we_ref, o_ref, oe_ref, h_scr = refs
    else:
        x_ref, g_ref, sh_ref, sc_ref, w_ref, o_ref, h_scr = refs

    @pl.when(pl.program_id(2) == 0)
    def _():
        x = _load_stream_tile(x_ref, nw, d)
        y = x * lax.rsqrt(jnp.mean(x * x, axis=-1, keepdims=True) + EPS) * g_ref[...]
        h = (y * (1.0 + sc_ref[...]) + sh_ref[...]).astype(BF16)
        h_scr[...] = h
        if has_extra:
            oe_ref[...] = _dot(h, we_ref[...])

    o_ref[...] = _dot(h_scr[...], w_ref[...]).astype(BF16)


def _in_proj(x, norm_g, shift, scale, w, w_extra, *, col_major):
    b, t, d = x.shape
    n = w.shape[1]
    tn = _pick(n, 1024, LANES)
    tm, nw, view, xblock, xmap = _stream_tiling(t, d, col_major, 1024)
    has_extra = w_extra is not None
    vec = pl.BlockSpec((None, 1, d), lambda bi, i, j: (bi, 0, 0))
    in_specs = [pl.BlockSpec(xblock, lambda bi, i, j: xmap(bi, i)),
                pl.BlockSpec((1, d), lambda bi, i, j: (0, 0)), vec, vec,
                pl.BlockSpec((d, tn), lambda bi, i, j: (0, j))]
    out_specs = [pl.BlockSpec((None, tm, tn), lambda bi, i, j: (bi, i, j))]
    out_shape = [jax.ShapeDtypeStruct((b, t, n), BF16)]
    args = [view(x), norm_g.reshape(1, d), shift, scale, w]
    if has_extra:
        in_specs.append(pl.BlockSpec((d, LANES), lambda bi, i, j: (0, 0)))
        out_specs.append(pl.BlockSpec((None, tm, LANES), lambda bi, i, j: (bi, i, 0)))
        out_shape.append(jax.ShapeDtypeStruct((b, t, LANES), F32))
        args.append(w_extra)
    outs = pl.pallas_call(
        functools.partial(_in_proj_kernel, nw=nw, d=d, has_extra=has_extra),
        grid=(b, t // tm, n // tn),
        in_specs=in_specs, out_specs=out_specs, out_shape=out_shape,
        scratch_shapes=[pltpu.VMEM((tm, d), BF16)],
        compiler_params=_cparams(("parallel", "parallel", "arbitrary")),
        name="in_proj",
    )(*args)
    return outs if has_extra else outs[0]


def _halo_specs(tm, t, width, col_block, tile_of):
    per = tm // HALO_BLK
    last = t // HALO_BLK - 1

    def cur(*g):
        bi, i = tile_of(*g)
        return (bi, i, col_block)

    def prev(*g):
        bi, i = tile_of(*g)
        return (bi, jnp.maximum(i * per - 1, 0), col_block)

    def nxt(*g):
        bi, i = tile_of(*g)
        return (bi, jnp.minimum((i + 1) * per, last), col_block)

    return [pl.BlockSpec((None, tm, width), cur),
            pl.BlockSpec((None, HALO_BLK, width), prev),
            pl.BlockSpec((None, HALO_BLK, width), nxt)]


def _fill_ext(ext, cur_ref, prev_ref, next_ref, i, nt, tm):
    ext[0:HALO, :] = jnp.where(i > 0, prev_ref[HALO_BLK - HALO:, :].astype(F32), 0.0)
    ext[HALO:HALO + tm, :] = cur_ref[...].astype(F32)
    ext[HALO + tm:, :] = jnp.where(i < nt - 1, next_ref[0:HALO, :].astype(F32), 0.0)


def _conv_taps(ext, w_ref, lanes, tm):
    acc = None
    for j in range(CONV_W):
        r0 = HALO - CONV_LEFT + j
        term = ext[r0:r0 + tm, lanes] * w_ref[j:j + 1, lanes]
        acc = term if acc is None else acc + term
    return acc


GDN_HPB = 2


XHALO = 16
GATE_GROUP = 8


def _gate_lane_sources():
    rep = V_HEADS // QK_HEADS
    src = np.zeros((QK_HEADS * GATE_GROUP,), np.int32)
    for dr in range(2):
        for kind in range(2):
            for head in range(V_HEADS):
                hg, vh = divmod(head, rep)
                src[hg * GATE_GROUP + dr * 2 * rep + kind * rep + vh] = dr * 2 * V_HEADS + kind * V_HEADS + head
    return src


def _gdn_in_kernel(x_ref, xp_ref, xn_ref, g_ref, sh_ref, sc_ref, w_ref, wz_ref, wab_ref, cw_ref, alog_ref, dtb_ref,
                   q_ref, k_ref, kt_ref, v_ref, z_ref, cols_ref, rows_ref, h_scr, y_scr, r_scr, *, tm, nw, d):
    i = pl.program_id(1)
    nt = pl.num_programs(1)
    j = pl.program_id(2)
    qk_w = QK_HEADS * HEAD_DIM
    tn = w_ref.shape[1]
    heads_per_tile = tn // HEAD_DIM

    def modulated(x):
        y = x * lax.rsqrt(jnp.mean(x * x, axis=-1, keepdims=True) + EPS) * g_ref[...]
        return (y * (1.0 + sc_ref[...]) + sh_ref[...]).astype(BF16)

    @pl.when(j == 0)
    def _():
        h = modulated(_load_stream_tile(x_ref, nw, d))
        h_scr[0:XHALO, :] = modulated(xp_ref[...])
        h_scr[XHALO:XHALO + tm, :] = h
        h_scr[XHALO + tm:, :] = modulated(xn_ref[...])
        ab = _dot(h, wab_ref[...])
        g = -jnp.exp(alog_ref[...]) * _softplus(ab + dtb_ref[...])
        beta = _sigmoid(ab)
        ri = lax.broadcasted_iota(jnp.int32, (CHUNK, CHUNK), 0)
        ci = lax.broadcasted_iota(jnp.int32, (CHUNK, CHUNK), 1)
        tril = (ri >= ci).astype(F32)
        triu = (ri <= ci).astype(F32)
        lane = lax.broadcasted_iota(jnp.int32, (CHUNK, LANES), 1)
        is_decay = (lane // 2) % 2 == 0
        is_bwd = (lane // 4) % 2 == 1
        for c in range(tm // CHUNK):
            rs = slice(c * CHUNK, (c + 1) * CHUNK)
            pre = jnp.dot(tril, g[rs], preferred_element_type=F32, precision=lax.Precision.HIGHEST)
            suf = jnp.dot(triu, g[rs], preferred_element_type=F32, precision=lax.Precision.HIGHEST)
            gate = jnp.where(is_decay, jnp.where(is_bwd, suf, pre), beta[rs])
            cols_ref[rs, :] = gate
            rows_ref[c] = gate.T

    y = _dot(h_scr[...], w_ref[...])
    z_ref[...] = _dot(h_scr[XHALO:XHALO + tm, :], wz_ref[...]).astype(BF16)
    y_scr[0:XHALO, :] = jnp.where(i > 0, y[0:XHALO], 0.0)
    y_scr[XHALO:XHALO + tm, :] = y[XHALO:XHALO + tm]
    y_scr[XHALO + tm:, :] = jnp.where(i < nt - 1, y[XHALO + tm:], 0.0)
    for h in range(heads_per_tile):
        lanes = slice(h * HEAD_DIM, (h + 1) * HEAD_DIM)
        acc = None
        for tap in range(CONV_W):
            r0 = XHALO - CONV_LEFT + tap
            term = y_scr[r0:r0 + tm, lanes] * cw_ref[tap:tap + 1, lanes]
            acc = term if acc is None else acc + term
        r_scr[:, lanes] = acc * _sigmoid(acc)

    def l2(yh):
        return yh * lax.rsqrt(jnp.sum(yh * yh, axis=-1, keepdims=True) + EPS)

    for jj in range(qk_w // tn):
        @pl.when(j == jj)
        def _(jj=jj):
            for h in range(heads_per_tile):
                lanes = slice(h * HEAD_DIM, (h + 1) * HEAD_DIM)
                q_ref[:, jj * tn + h * HEAD_DIM:jj * tn + (h + 1) * HEAD_DIM] = (
                    l2(r_scr[:, lanes]) * (HEAD_DIM ** -0.5)).astype(BF16)

    for jj in range(qk_w // tn):
        @pl.when(j == qk_w // tn + jj)
        def _(jj=jj):
            for h in range(heads_per_tile):
                kn = l2(r_scr[:, h * HEAD_DIM:(h + 1) * HEAD_DIM])
                head = jj * heads_per_tile + h
                k_ref[:, head * HEAD_DIM:(head + 1) * HEAD_DIM] = kn.astype(BF16)
                for c in range(tm // CHUNK):
                    kc = kn[c * CHUNK:(c + 1) * CHUNK]
                    kt_ref[head, c] = jnp.concatenate([kc, kc], axis=0).T.astype(BF16)

    for jj in range(V_HEADS * HEAD_DIM // tn):
        @pl.when(j == 2 * qk_w // tn + jj)
        def _(jj=jj):
            v_ref[:, jj * tn:(jj + 1) * tn] = r_scr[...].astype(BF16)


def _gdn_in(x, norm_g, shift, scale, w_main, w_ab, conv_w, a_log, dt_bias, *, col_major):
    b, t, d = x.shape
    qk_w = QK_HEADS * HEAD_DIM
    v_w = V_HEADS * HEAD_DIM
    tn = 1024
    assert w_main.shape[1] == 2 * qk_w + 2 * v_w and qk_w % tn == 0 and v_w % tn == 0
    nj = (2 * qk_w + v_w) // tn
    tz = v_w // nj
    w_qkv, w_z = w_main[:, :2 * qk_w + v_w], w_main[:, 2 * qk_w + v_w:]
    tm, nw, view, xblock, xmap = _stream_tiling(t, d, col_major, 512)
    nt = t // tm
    if col_major:
        rows = t // GRID_W
        assert rows % XHALO == 0
        prev = lambda bi, i, j: (bi, rows // XHALO - 1, jnp.maximum(i * nw - 1, 0))
        nxt = lambda bi, i, j: (bi, 0, jnp.minimum((i + 1) * nw, GRID_W - 1))
    else:
        per = tm // XHALO
        prev = lambda bi, i, j: (bi, jnp.maximum(i * per - 1, 0), 0)
        nxt = lambda bi, i, j: (bi, jnp.minimum((i + 1) * per, t // XHALO - 1), 0)
    src = _gate_lane_sources()
    pad_lanes = lambda a: jnp.pad(a, ((0, 0), (0, LANES - a.shape[1])))
    zeros = jnp.zeros((2, V_HEADS), F32)
    lane_vec = lambda p: pad_lanes(jnp.stack([p, zeros], axis=1).reshape(1, 4 * V_HEADS)[:, src])
    w_gate = pad_lanes(w_ab[:, src])
    vec = pl.BlockSpec((None, 1, d), lambda bi, i, j: (bi, 0, 0))
    const = lambda shape: pl.BlockSpec(shape, lambda bi, i, j: (0,) * len(shape))
    tile = lambda width: pl.BlockSpec((None, tm, width), lambda bi, i, j: (bi, i, 0))
    xv = view(x)
    return pl.pallas_call(
        functools.partial(_gdn_in_kernel, tm=tm, nw=nw, d=d),
        grid=(b, nt, nj),
        in_specs=[pl.BlockSpec(xblock, lambda bi, i, j: xmap(bi, i)),
                  pl.BlockSpec((None, XHALO, d), prev), pl.BlockSpec((None, XHALO, d), nxt),
                  const((1, d)), vec, vec,
                  pl.BlockSpec((d, tn), lambda bi, i, j: (0, j)),
                  pl.BlockSpec((d, tz), lambda bi, i, j: (0, j)),
                  const((d, LANES)), pl.BlockSpec((CONV_W, tn), lambda bi, i, j: (0, j)),
                  const((1, LANES)), const((1, LANES))],
        out_specs=[tile(qk_w), tile(qk_w),
                   pl.BlockSpec((None, QK_HEADS, tm // CHUNK, HEAD_DIM, 2 * CHUNK), lambda bi, i, j: (bi, 0, i, 0, 0)),
                   tile(v_w), pl.BlockSpec((None, tm, tz), lambda bi, i, j: (bi, i, j)), tile(LANES),
                   pl.BlockSpec((None, tm // CHUNK, LANES, CHUNK), lambda bi, i, j: (bi, i, 0, 0))],
        out_shape=[jax.ShapeDtypeStruct((b, t, qk_w), BF16), jax.ShapeDtypeStruct((b, t, qk_w), BF16),
                   jax.ShapeDtypeStruct((b, QK_HEADS, t // CHUNK, HEAD_DIM, 2 * CHUNK), BF16),
                   jax.ShapeDtypeStruct((b, t, v_w), BF16), jax.ShapeDtypeStruct((b, t, v_w), BF16),
                   jax.ShapeDtypeStruct((b, t, LANES), F32),
                   jax.ShapeDtypeStruct((b, t // CHUNK, LANES, CHUNK), F32)],
        scratch_shapes=[pltpu.VMEM((tm + 2 * XHALO, d), BF16), pltpu.VMEM((tm + 2 * XHALO, tn), F32),
                        pltpu.VMEM((tm, tn), F32)],
        compiler_params=_cparams(("parallel", "parallel", "arbitrary")),
        name="gdn_in",
    )(xv, xv, xv, norm_g.reshape(1, d), shift, scale, w_qkv, w_z, w_gate, conv_w, lane_vec(a_log), lane_vec(dt_bias))


INV_BASE = 8


def _block_diag2(pair, first):
    return jnp.concatenate([jnp.where(first, pair, 0.0), jnp.where(first, 0.0, pair)], axis=0).astype(BF16)


SCAN_STAGES = 13
SCAN_SKEW = 2
SCAN_CHUNKS = 16


def _gdn_scan_kernel(qf_ref, kf_ref, tf_ref, vf_ref, cf_ref, rf_ref, qb_ref, kb_ref, tb_ref, vb_ref, cb_ref,
                     rb_ref, s0_ref, of_ref, ob_ref, sout_ref, s_scr, *, cpb, hpb):
    n = pl.program_id(2)

    @pl.when(n == 0)
    def _():
        s_scr[...] = s0_ref[...]

    ri = lax.broadcasted_iota(jnp.int32, (CHUNK, 2 * CHUNK), 0)
    lane = lax.broadcasted_iota(jnp.int32, (CHUNK, 2 * CHUNK), 1)
    first = lane < CHUNK
    first_row = first[0:1]
    ci = jnp.where(first, lane, lane - CHUNK)
    eye = (ri == ci).astype(F32)
    incl = (ri >= ci, ri <= ci)
    strict = (ri > ci, ri < ci)
    same_block = lambda size: (ri // size) == (ci // size)
    refs = ((qf_ref, kf_ref, tf_ref, vf_ref, cf_ref, rf_ref, of_ref),
            (qb_ref, kb_ref, tb_ref, vb_ref, cb_ref, rb_ref, ob_ref))
    pairs = [(hl, d) for hl in range(hpb) for d in range(2)]
    hd = HEAD_DIM
    bd = lambda xs: [_block_diag2(x, first) for x in xs]
    zero_s = jnp.zeros((hd, hd), BF16)
    zero_v = jnp.zeros((CHUNK, hd), BF16)
    state = [s_scr[hl, d] for hl, d in pairs]
    first_gate = pl.program_id(1) * (hpb * GATE_GROUP)

    def chunk_stages(c):
        probs = []
        for hl, d in pairs:
            cc = c if d == 0 else cpb - 1 - c
            probs.append((hl, d, cc, slice(cc * CHUNK, (cc + 1) * CHUNK)))
        idx = range(len(probs))
        gram = [_dot(jnp.concatenate([refs[d][0][sl, hl * hd:(hl + 1) * hd],
                                      refs[d][1][sl, hl * hd:(hl + 1) * hd]], axis=0), refs[d][2][hl, cc])
                for hl, d, cc, sl in probs]
        yield
        lmat, lhs2, qd, scale_u, scale_w, gl = [], [], [], [], [], []
        for j, (hl, d, cc, sl) in enumerate(probs):
            q_ref, k_ref, t_ref, v_ref, c_ref, r_ref, _ = refs[d]
            cols = pltpu.roll(c_ref[sl, :], lax.rem(LANES - first_gate, LANES), 1)
            rows = r_ref[cc, pl.ds(pl.multiple_of(first_gate, GATE_GROUP), hpb * GATE_GROUP), :]
            g0 = hl * GATE_GROUP + d * 4
            pair_row = lambda r: jnp.concatenate([rows[r:r + 1, :], rows[r + 1:r + 2, :]], axis=1)
            grow = pair_row(g0)
            brow = pair_row(g0 + 2)
            gcols = [cols[:, g0 + vh:g0 + vh + 1] for vh in range(2)]
            bcols = [cols[:, g0 + 2 + vh:g0 + 3 + vh] for vh in range(2)]
            gtots = [g[CHUNK - 1:CHUNK, :] if d == 0 else g[0:1, :] for g in gcols]
            gpair = jnp.where(first, gcols[0], gcols[1])
            bpair = jnp.where(first, bcols[0], bcols[1])
            decay = jnp.where(incl[d], jnp.exp(jnp.where(incl[d], gpair - grow, 0.0)), 0.0)
            lmat.append(jnp.where(strict[d], gram[j][CHUNK:] * bpair * decay, 0.0))
            qkm = gram[j][:CHUNK] * decay
            kdt = t_ref[hl, cc].astype(F32) * jnp.exp(jnp.where(first_row, gtots[0], gtots[1]) - grow)
            lhs2.append(jnp.concatenate([qkm, kdt], axis=0).astype(BF16))
            q = q_ref[sl, hl * hd:(hl + 1) * hd].astype(F32)
            qd.append(jnp.concatenate([q * jnp.exp(g) for g in gcols], axis=1))
            scale_u.append(brow)
            scale_w.append(brow * jnp.exp(grow))
            gl.append(jnp.concatenate([jnp.broadcast_to(jnp.exp(g), (1, hd)) for g in gtots], axis=1))
        base = same_block(INV_BASE)
        p = [-jnp.where(base, l, 0.0) for l in lmat]
        p2 = [_dot(p[i].astype(BF16), m) for i, m in enumerate(bd(p))]
        yield
        p2_bd = bd(p2)
        p4 = [_dot(p2[i].astype(BF16), p2_bd[i]) for i in idx]
        tinv = [eye + x for x in p]
        tinv = [tinv[i] + _dot(tinv[i].astype(BF16), p2_bd[i]) for i in idx]
        yield
        tinv = [tinv[i] + _dot(tinv[i].astype(BF16), m) for i, m in enumerate(bd(p4))]
        yield
        size = INV_BASE
        while size < CHUNK:
            off_diag = same_block(2 * size) & jnp.logical_not(same_block(size))
            a16 = [jnp.where(off_diag, l, 0.0).astype(BF16) for l in lmat]
            at = [_dot(a16[i], m) for i, m in enumerate(bd(tinv))]
            yield
            tinv = [tinv[i] - _dot(tinv[i].astype(BF16), m) for i, m in enumerate(bd(at))]
            yield
            size *= 2
        u, w = [], []
        for j, (hl, d, cc, sl) in enumerate(probs):
            k16 = refs[d][1][sl, hl * hd:(hl + 1) * hd]
            v16 = [refs[d][3][sl, (hl * 2 + vh) * hd:(hl * 2 + vh + 1) * hd] for vh in range(2)]
            v_bd16 = jnp.concatenate([jnp.concatenate([v16[0], zero_v], axis=1),
                                      jnp.concatenate([zero_v, v16[1]], axis=1)], axis=0)
            k_bd16 = jnp.concatenate([jnp.concatenate([k16, zero_v], axis=1),
                                      jnp.concatenate([zero_v, k16], axis=1)], axis=0)
            u.append(_dot((tinv[j] * scale_u[j]).astype(BF16), v_bd16))
            w.append(_dot((tinv[j] * scale_w[j]).astype(BF16), k_bd16))
        yield
        wq16 = [jnp.concatenate([w[j], qd[j]], axis=0).astype(BF16) for j in idx]
        s16 = [x.astype(BF16) for x in state]
        s_bd = [jnp.concatenate([jnp.concatenate([x[:, :hd], zero_s], axis=1),
                                 jnp.concatenate([zero_s, x[:, hd:]], axis=1)], axis=0) for x in s16]
        ws = [_dot(wq16[i], s_bd[i]) for i in idx]
        yield
        vnew16 = [(u[i] - ws[i][:CHUNK]).astype(BF16) for i in idx]
        v_bd = [jnp.concatenate([jnp.concatenate([x[:, :hd], zero_v], axis=1),
                                 jnp.concatenate([zero_v, x[:, hd:]], axis=1)], axis=0) for x in vnew16]
        upd = [_dot(lhs2[i], v_bd[i]) for i in idx]
        for i, (hl, d, cc, sl) in enumerate(probs):
            state[i] = state[i] * gl[i] + upd[i][CHUNK:]
            refs[d][6][sl, hl * 2 * hd:(hl + 1) * 2 * hd] = (ws[i][CHUNK:] + upd[i][:CHUNK]).astype(BF16)

    gens = [chunk_stages(c) for c in range(cpb)]
    for tick in range(SCAN_STAGES + SCAN_SKEW * (cpb - 1)):
        for c in range(cpb):
            if 0 <= tick - SCAN_SKEW * c < SCAN_STAGES:
                next(gens[c], None)
    for i, (hl, d) in enumerate(pairs):
        s_scr[hl, d] = state[i]

    @pl.when(n == pl.num_programs(2) - 1)
    def _():
        sout_ref[...] = s_scr[...]


def _gdn_scan(qn, kn, kt, v, cols, rows, s0):
    b, t, _ = qn.shape
    rep = V_HEADS // QK_HEADS
    assert rep == 2
    hpb = GDN_HPB
    blk = _pick(t, SCAN_CHUNKS * CHUNK, CHUNK)
    cpb = blk // CHUNK
    nb = t // blk
    fwd = lambda n: n
    bwd = lambda n: nb - 1 - n

    def specs(order):
        return [pl.BlockSpec((None, blk, hpb * HEAD_DIM), lambda bi, h, n: (bi, order(n), h)),
                pl.BlockSpec((None, blk, hpb * HEAD_DIM), lambda bi, h, n: (bi, order(n), h)),
                pl.BlockSpec((None, hpb, cpb, HEAD_DIM, rep * CHUNK), lambda bi, h, n: (bi, h, order(n), 0, 0)),
                pl.BlockSpec((None, blk, hpb * rep * HEAD_DIM), lambda bi, h, n: (bi, order(n), h)),
                pl.BlockSpec((None, blk, LANES), lambda bi, h, n: (bi, order(n), 0)),
                pl.BlockSpec((None, cpb, LANES, CHUNK), lambda bi, h, n: (bi, order(n), 0, 0))]

    state_block = (None, hpb, 2, HEAD_DIM, rep * HEAD_DIM)
    state_spec = pl.BlockSpec(state_block, lambda bi, h, n: (bi, h, 0, 0, 0))
    o_shape = jax.ShapeDtypeStruct((b, t, V_HEADS * HEAD_DIM), BF16)
    return pl.pallas_call(
        functools.partial(_gdn_scan_kernel, cpb=cpb, hpb=hpb),
        grid=(b, QK_HEADS // hpb, nb),
        in_specs=specs(fwd) + specs(bwd) + [state_spec],
        out_specs=[pl.BlockSpec((None, blk, hpb * rep * HEAD_DIM), lambda bi, h, n: (bi, fwd(n), h)),
                   pl.BlockSpec((None, blk, hpb * rep * HEAD_DIM), lambda bi, h, n: (bi, bwd(n), h)),
                   state_spec],
        out_shape=[o_shape, o_shape, jax.ShapeDtypeStruct(s0.shape, F32)],
        scratch_shapes=[pltpu.VMEM(state_block[1:], F32)],
        compiler_params=_cparams(("parallel", "parallel", "arbitrary")),
        name="gdn_scan",
    )(qn, kn, kt, v, cols, rows, qn, kn, kt, v, cols, rows, s0)


def _lru_scan_kernel(fc_ref, fp_ref, fn_ref, bc_ref, bp_ref, bn_ref, cw_ref, cb_ref, wr_ref, br_ref,
                     wi_ref, bi_ref, lam_ref, h0_ref, hf_ref, hb_ref, hfin_ref, ext, carry, *, tm):
    i = pl.program_id(1)
    nt = pl.num_programs(1)

    @pl.when(i == 0)
    def _():
        carry[...] = h0_ref[...]

    width = cw_ref.shape[1]
    bs = width // LRU_BLOCKS
    nv = tm // SUB
    tpb = bs // LANES
    sub = lax.broadcasted_iota(jnp.int32, (SUB, bs), 0)
    dirs = ((fc_ref, fp_ref, fn_ref, hf_ref, i), (bc_ref, bp_ref, bn_ref, hb_ref, nt - 1 - i))
    for d, (c_ref, p_ref, n_ref, o_ref, tile) in enumerate(dirs):
        for lt in range(width // LANES):
            lt_lanes = slice(lt * LANES, (lt + 1) * LANES)
            ext[lt, 0:HALO, :] = jnp.where(tile > 0, p_ref[HALO_BLK - HALO:, lt_lanes].astype(F32), 0.0)
            ext[lt, HALO:HALO + tm, :] = c_ref[:, lt_lanes].astype(F32)
            ext[lt, HALO + tm:, :] = jnp.where(tile < nt - 1, n_ref[0:HALO, lt_lanes].astype(F32), 0.0)
        for blk in range(LRU_BLOCKS):
            lanes = slice(blk * bs, (blk + 1) * bs)
            groups = []
            for v in range(nv):
                pieces = []
                for lt in range(blk * tpb, (blk + 1) * tpb):
                    acc = None
                    for j in range(CONV_W):
                        term = (ext[lt, pl.ds(HALO - CONV_LEFT + j + v, SUB, stride=nv), :]
                                * cw_ref[j:j + 1, lt * LANES:(lt + 1) * LANES])
                        acc = term if acc is None else acc + term
                    pieces.append(acc)
                groups.append(jnp.concatenate(pieces, axis=1))
            xc = jnp.concatenate(groups, axis=0) + cb_ref[:, lanes]
            xc16 = xc.astype(BF16)
            r = _sigmoid(_dot(xc16, wr_ref[d, blk]) + br_ref[d:d + 1, lanes])
            gi = _sigmoid(_dot(xc16, wi_ref[d, blk]) + bi_ref[d:d + 1, lanes])
            log_a = -RG_C * r * _softplus(-lam_ref[d:d + 1, lanes])
            a = jnp.exp(log_a)
            bv = jnp.sqrt(1.0 - jnp.exp(2.0 * log_a)) * (gi * xc)
            order = range(nv) if d == 0 else range(nv - 1, -1, -1)
            hs, acums = [None] * nv, [None] * nv
            h_run = a_run = None
            for v in order:
                av, bvv = a[v * SUB:(v + 1) * SUB], bv[v * SUB:(v + 1) * SUB]
                h_run = bvv if h_run is None else av * h_run + bvv
                a_run = av if a_run is None else av * a_run
                hs[v], acums[v] = h_run, a_run
            e_end, p_end = h_run, a_run
            s = 1
            while s < SUB:
                shift = s if d == 0 else SUB - s
                keep = (sub >= s) if d == 0 else (sub < SUB - s)
                p_prev = jnp.where(keep, pltpu.roll(p_end, shift, 0), 1.0)
                e_prev = jnp.where(keep, pltpu.roll(e_end, shift, 0), 0.0)
                e_end = p_end * e_prev + e_end
                p_end = p_end * p_prev
                s *= 2
            cin = jnp.broadcast_to(carry[d, 0:1, lanes], (SUB, bs))
            one = 1 if d == 0 else SUB - 1
            first_sub = (sub == 0) if d == 0 else (sub == SUB - 1)
            start = jnp.where(first_sub, cin, pltpu.roll(p_end, one, 0) * cin + pltpu.roll(e_end, one, 0))
            for v in range(nv):
                h = acums[v] * start + hs[v]
                for k in range(tpb):
                    o_ref[blk * tpb + k, pl.ds(v, SUB, stride=nv), :] = h[:, k * LANES:(k + 1) * LANES]
            last = SUB - 1 if d == 0 else 0
            carry[d, 0:1, lanes] = (p_end * cin + e_end)[last:last + 1, :]

    @pl.when(i == nt - 1)
    def _():
        hfin_ref[...] = carry[...]


def _lru_scan(proj, conv_w, conv_b, w_r, b_r, w_i, b_i, lam, h0):
    b, t, two_w = proj.shape
    width = two_w // 2
    tm = _pick(t, 256, CHUNK)
    nt = t // tm
    nlt = width // LANES
    whole = lambda a: pl.BlockSpec(a.shape, lambda bi, i: (0,) * a.ndim)
    state_spec = pl.BlockSpec((None, 2, 1, width), lambda bi, i: (bi, 0, 0, 0))
    consts = [conv_w, conv_b.reshape(1, width), w_r, b_r, w_i, b_i, lam]
    h_shape = jax.ShapeDtypeStruct((b, nlt, t, LANES), F32)
    return pl.pallas_call(
        functools.partial(_lru_scan_kernel, tm=tm),
        grid=(b, nt),
        in_specs=(_halo_specs(tm, t, width, 0, lambda bi, i: (bi, i))
                  + _halo_specs(tm, t, width, 0, lambda bi, i: (bi, nt - 1 - i))
                  + [whole(a) for a in consts] + [state_spec]),
        out_specs=[pl.BlockSpec((None, nlt, tm, LANES), lambda bi, i: (bi, 0, i, 0)),
                   pl.BlockSpec((None, nlt, tm, LANES), lambda bi, i: (bi, 0, nt - 1 - i, 0)),
                   state_spec],
        out_shape=[h_shape, h_shape, jax.ShapeDtypeStruct(h0.shape, F32)],
        scratch_shapes=[pltpu.VMEM((nlt, tm + 2 * HALO, LANES), F32),
                        pltpu.VMEM((2, 1, width), F32)],
        compiler_params=_cparams(("parallel", "arbitrary")),
        name="lru_scan",
    )(proj, proj, proj, proj, proj, proj, *consts, h0)


def _residual_store(xo_ref, x_ref, y, gate, fg_ref, nw, d, rows):
    for k in range(nw):
        lanes = slice(k * d, (k + 1) * d) if nw > 1 else slice(None)
        yk = y[k * rows:(k + 1) * rows] if nw > 1 else y
        xn = x_ref[:, lanes] + gate * yk
        if fg_ref is not None:
            xn = xn * lax.rsqrt(jnp.mean(xn * xn, axis=-1, keepdims=True) + EPS) * fg_ref[...]
        xo_ref[:, lanes] = xn


def _gdn_out_kernel(*refs, nw, d, rows, final):
    of_ref, ob_ref, z_ref, ng_ref, w_ref, x_ref, gate_ref = refs[:7]
    fg_ref = refs[7] if final else None
    xo_ref = refs[-1]
    parts = []
    for h in range(V_HEADS):
        lanes = slice(h * HEAD_DIM, (h + 1) * HEAD_DIM)
        o = of_ref[:, lanes].astype(F32) + ob_ref[:, lanes].astype(F32)
        o = o * lax.rsqrt(jnp.mean(o * o, axis=-1, keepdims=True) + EPS) * ng_ref[...]
        z = z_ref[:, lanes].astype(F32)
        parts.append((o * (z * _sigmoid(z))).astype(BF16))
    y = _dot(jnp.concatenate(parts, axis=1), w_ref[...])
    _residual_store(xo_ref, x_ref, y, gate_ref[...], fg_ref, nw, d, rows)


def _lru_out_kernel(*refs, nw, d, rows, final):
    hf_ref, hb_ref, gt_ref, w_ref, x_ref, gate_ref = refs[:6]
    fg_ref = refs[6] if final else None
    xo_ref = refs[-1]
    gt = gt_ref[...].astype(F32)
    h = jnp.concatenate([hf_ref[lt] + hb_ref[lt] for lt in range(hf_ref.shape[0])], axis=1)
    hg = (h * (gt * _sigmoid(gt))).astype(BF16)
    y = _dot(hg, w_ref[...])
    _residual_store(xo_ref, x_ref, y, gate_ref[...], fg_ref, nw, d, rows)


def _out_proj(kind, branch, gate_src, gate_col, extra, w_out, x, gate, final_g, *, col_major):
    b, t, d = x.shape
    wb = w_out.shape[0]
    tm, nw, view, xblock, xmap = _stream_tiling(t, d, col_major, 256)
    tile = lambda col: pl.BlockSpec((None, tm, wb), lambda bi, i: (bi, i, col))
    if kind == "gdn":
        in_specs = [tile(0), tile(0), tile(gate_col), pl.BlockSpec((1, HEAD_DIM), lambda bi, i: (0, 0))]
        args = [branch[0], branch[1], gate_src, extra.reshape(1, HEAD_DIM)]
        body = _gdn_out_kernel
    else:
        lane_tiles = pl.BlockSpec((None, wb // LANES, tm, LANES), lambda bi, i: (bi, 0, i, 0))
        in_specs = [lane_tiles, lane_tiles, tile(gate_col)]
        args = [branch[0], branch[1], gate_src]
        body = _lru_out_kernel
    in_specs += [pl.BlockSpec(w_out.shape, lambda bi, i: (0, 0)),
                 pl.BlockSpec(xblock, lambda bi, i: xmap(bi, i)),
                 pl.BlockSpec((None, 1, d), lambda bi, i: (bi, 0, 0))]
    args += [w_out, view(x), gate]
    final = final_g is not None
    if final:
        in_specs.append(pl.BlockSpec((1, d), lambda bi, i: (0, 0)))
        args.append(final_g.reshape(1, d))
    xv = view(x)
    out = pl.pallas_call(
        functools.partial(body, nw=nw, d=d, rows=tm // nw, final=final),
        grid=(b, t // tm),
        in_specs=in_specs,
        out_specs=pl.BlockSpec(xblock, lambda bi, i: xmap(bi, i)),
        out_shape=jax.ShapeDtypeStruct(xv.shape, F32),
        compiler_params=_cparams(("parallel", "parallel")),
        name=kind + "_out",
    )(*args)
    return out.reshape(b, t, d)


def _gdn_layer(streams, mods, norm_g, w_in, conv_w, a_log, dt_bias, head_g, w_out, *, col_major, final_g,
               update_ctx):
    qkvz_w = 2 * QK_HEADS * HEAD_DIM + 2 * V_HEADS * HEAD_DIM
    w_main = w_in[:, :qkvz_w].astype(BF16)
    w_ab = jnp.pad(w_in[:, qkvz_w:], ((0, 0), (0, LANES - (w_in.shape[1] - qkvz_w)))).astype(BF16)
    w_out16 = w_out.astype(BF16)
    state = None
    outs = []
    for (x, cm, fg, need_y), (shift, scale, gate) in zip(
            ((streams[0], False, None, update_ctx), (streams[1], col_major, final_g, True)), mods):
        b = x.shape[0]
        if state is None:
            state = jnp.zeros((b, QK_HEADS, 2, HEAD_DIM, V_HEADS // QK_HEADS * HEAD_DIM), F32)
        qn, kn, kt, v, z, cols, rows = _gdn_in(x, norm_g, shift, scale, w_main, w_ab, conv_w, a_log, dt_bias,
                                               col_major=cm)
        o_f, o_b, state = _gdn_scan(qn, kn, kt, v, cols, rows, state)
        if need_y:
            x = _out_proj("gdn", (o_f, o_b), z, 0, head_g, w_out16, x, gate, fg, col_major=cm)
        outs.append(x)
    return outs


def _lru_layer(streams, mods, norm_g, w_in, conv_w, conv_b, w_r, b_r, w_i, b_i, lam, w_out, *, col_major,
               final_g, update_ctx):
    w_in16 = w_in.astype(BF16)
    w_r16 = w_r.astype(BF16)
    w_i16 = w_i.astype(BF16)
    w_out16 = w_out.astype(BF16)
    state = None
    outs = []
    for (x, cm, fg, need_y), (shift, scale, gate) in zip(
            ((streams[0], False, None, update_ctx), (streams[1], col_major, final_g, True)), mods):
        b = x.shape[0]
        if state is None:
            state = jnp.zeros((b, 2, 1, conv_w.shape[1]), F32)
        proj = _in_proj(x, norm_g, shift, scale, w_in16, None, col_major=cm)
        h_f, h_b, state = _lru_scan(proj, conv_w, conv_b, w_r16, b_r, w_i16, b_i, lam, state)
        if need_y:
            x = _out_proj("lru", (h_f, h_b), proj, 1, None, w_out16, x, gate, fg, col_major=cm)
        outs.append(x)
    return outs


def kernel(x, c, ctx, c_ctx, mod_w, mod_b, norm_g, gdn_w_in, gdn_conv, gdn_a_log, gdn_dt_bias, gdn_norm_g,
           gdn_w_out, lru_w_in, lru_conv_w, lru_conv_b, lru_w_r, lru_b_r, lru_w_i, lru_b_i, lru_lambda,
           lru_w_out, final_g):
    b, _, d = x.shape
    depth = mod_w.shape[0]
    assert b + 1 <= MOD_ROWS
    cond = jnp.concatenate([c, c_ctx[None], jnp.zeros((MOD_ROWS - b - 1, d), F32)], axis=0)
    mod = _modulation(cond, mod_w, mod_b)
    for i in range(depth):
        lat = [mod[i, :b, k * d:(k + 1) * d].reshape(b, 1, d) for k in range(3)]
        cmod = [jnp.broadcast_to(mod[i, b, k * d:(k + 1) * d].reshape(1, 1, d), (b, 1, d)) for k in range(3)]
        col_major = (i + i // 2) % 2 == 1
        last = i == depth - 1
        j = i // 2
        common = dict(col_major=col_major, final_g=final_g if last else None, update_ctx=not last)
        if i % 2 == 0:
            ctx, x = _gdn_layer((ctx, x), (cmod, lat), norm_g[i], gdn_w_in[j], gdn_conv[j], gdn_a_log[j],
                                gdn_dt_bias[j], gdn_norm_g[j], gdn_w_out[j], **common)
        else:
            ctx, x = _lru_layer((ctx, x), (cmod, lat), norm_g[i], lru_w_in[j], lru_conv_w[j], lru_conv_b[j],
                                lru_w_r[j], lru_b_r[j], lru_w_i[j], lru_b_i[j], lru_lambda[j], lru_w_out[j],
                                **common)
    return x
```

```python
import functools

import numpy as np
import jax
import jax.numpy as jnp
from jax import lax
from jax.experimental import pallas as pl
from jax.experimental.pallas import tpu as pltpu

F32 = jnp.float32
BF16 = jnp.bfloat16

EPS = 1e-6
GRID_W = 64
HEAD_DIM = 128
QK_HEADS = 8
V_HEADS = 16
CHUNK = 64
CONV_W = 4
CONV_LEFT = 2
HALO = 8
HALO_BLK = 16
LRU_BLOCKS = 4
RG_C = 8.0
LANES = 128
SUB = 8
MOD_ROWS = 8

VMEM_LIMIT = 56 * 1024 * 1024


def _cparams(sem):
    return pltpu.CompilerParams(dimension_semantics=sem, vmem_limit_bytes=VMEM_LIMIT)


def _pick(total, target, mult):
    best = None
    for cand in range(mult, min(total, target) + 1, mult):
        if total % cand == 0:
            best = cand
    assert best is not None, (total, target, mult)
    return best


def _sigmoid(x):
    return 1.0 / (1.0 + jnp.exp(-x))


def _softplus(x):
    return jnp.maximum(x, 0.0) + jnp.log(1.0 + jnp.exp(-jnp.abs(x)))


def _dot(a, b):
    return jnp.dot(a, b, preferred_element_type=F32)


def _dot_nt(a, b):
    return lax.dot_general(a, b, (((1,), (1,)), ((), ())), preferred_element_type=F32)


def _dot_tn(a, b):
    return lax.dot_general(a, b, (((0,), (0,)), ((), ())), preferred_element_type=F32)


def _mod_kernel(c_ref, w_ref, b_ref, o_ref):
    c = c_ref[...]
    sc = c * _sigmoid(c)
    o_ref[...] = jnp.dot(sc, w_ref[...], preferred_element_type=F32,
                         precision=lax.Precision.HIGHEST) + b_ref[...]


def _modulation(cond, mod_w, mod_b):
    depth, d, n3 = mod_w.shape
    return pl.pallas_call(
        _mod_kernel,
        grid=(depth, n3 // d),
        in_specs=[pl.BlockSpec((MOD_ROWS, d), lambda i, j: (0, 0)),
                  pl.BlockSpec((None, d, d), lambda i, j: (i, 0, j)),
                  pl.BlockSpec((None, 1, d), lambda i, j: (i, 0, j))],
        out_specs=pl.BlockSpec((None, MOD_ROWS, d), lambda i, j: (i, 0, j)),
        out_shape=jax.ShapeDtypeStruct((depth, MOD_ROWS, n3), F32),
        compiler_params=_cparams(("parallel", "parallel")),
        name="modulation",
    )(cond, mod_w, mod_b.reshape(depth, 1, n3))


def _stream_tiling(t, d, col_major, target):
    if not col_major:
        tm = _pick(t, target, CHUNK)
        return tm, 1, (lambda a: a), (None, tm, d), (lambda b, i: (b, i, 0))
    rows = t // GRID_W
    assert rows * GRID_W == t and rows % 8 == 0
    nw = _pick(GRID_W, max(target // rows, 1), 1)
    tm = rows * nw
    return (tm, nw, (lambda a: a.reshape(a.shape[0], rows, GRID_W * d)),
            (None, rows, nw * d), (lambda b, i: (b, 0, i)))


def _load_stream_tile(x_ref, nw, d):
    if nw == 1:
        return x_ref[...]
    return jnp.concatenate([x_ref[:, k * d:(k + 1) * d] for k in range(nw)], axis=0)


def _in_proj_kernel(*refs, nw, d, has_extra):
    if has_extra:
        x_ref, g_ref, sh_ref, sc_ref, w_ref, we_ref, o_ref, oe_ref, h_scr = refs
    else:
        x_ref, g_ref, sh_ref, sc_ref, w_ref, o_ref, h_scr = refs

    @pl.when(pl.program_id(2) == 0)
    def _():
        x = _load_stream_tile(x_ref, nw, d)
        y = x * lax.rsqrt(jnp.mean(x * x, axis=-1, keepdims=True) + EPS) * g_ref[...]
        h = (y * (1.0 + sc_ref[...]) + sh_ref[...]).astype(BF16)
        h_scr[...] = h
        if has_extra:
            oe_ref[...] = _dot(h, we_ref[...])

    o_ref[...] = _dot(h_scr[...], w_ref[...]).astype(BF16)


def _in_proj(x, norm_g, shift, scale, w, w_extra, *, col_major):
    b, t, d = x.shape
    n = w.shape[1]
    tn = _pick(n, 1024, LANES)
    tm, nw, view, xblock, xmap = _stream_tiling(t, d, col_major, 1024)
    has_extra = w_extra is not None
    vec = pl.BlockSpec((None, 1, d), lambda bi, i, j: (bi, 0, 0))
    in_specs = [pl.BlockSpec(xblock, lambda bi, i, j: xmap(bi, i)),
                pl.BlockSpec((1, d), lambda bi, i, j: (0, 0)), vec, vec,
                pl.BlockSpec((d, tn), lambda bi, i, j: (0, j))]
    out_specs = [pl.BlockSpec((None, tm, tn), lambda bi, i, j: (bi, i, j))]
    out_shape = [jax.ShapeDtypeStruct((b, t, n), BF16)]
    args = [view(x), norm_g.reshape(1, d), shift, scale, w]
    if has_extra:
        in_specs.append(pl.BlockSpec((d, LANES), lambda bi, i, j: (0, 0)))
        out_specs.append(pl.BlockSpec((None, tm, LANES), lambda bi, i, j: (bi, i, 0)))
        out_shape.append(jax.ShapeDtypeStruct((b, t, LANES), F32))
        args.append(w_extra)
    outs = pl.pallas_call(
        functools.partial(_in_proj_kernel, nw=nw, d=d, has_extra=has_extra),
        grid=(b, t // tm, n // tn),
        in_specs=in_specs, out_specs=out_specs, out_shape=out_shape,
        scratch_shapes=[pltpu.VMEM((tm, d), BF16)],
        compiler_params=_cparams(("parallel", "parallel", "arbitrary")),
        name="in_proj",
    )(*args)
    return outs if has_extra else outs[0]


def _halo_specs(tm, t, width, col_block, tile_of):
    per = tm // HALO_BLK
    last = t // HALO_BLK - 1

    def cur(*g):
        bi, i = tile_of(*g)
        return (bi, i, col_block)

    def prev(*g):
        bi, i = tile_of(*g)
        return (bi, jnp.maximum(i * per - 1, 0), col_block)

    def nxt(*g):
        bi, i = tile_of(*g)
        return (bi, jnp.minimum((i + 1) * per, last), col_block)

    return [pl.BlockSpec((None, tm, width), cur),
            pl.BlockSpec((None, HALO_BLK, width), prev),
            pl.BlockSpec((None, HALO_BLK, width), nxt)]


def _fill_ext(ext, cur_ref, prev_ref, next_ref, i, nt, tm):
    ext[0:HALO, :] = jnp.where(i > 0, prev_ref[HALO_BLK - HALO:, :].astype(F32), 0.0)
    ext[HALO:HALO + tm, :] = cur_ref[...].astype(F32)
    ext[HALO + tm:, :] = jnp.where(i < nt - 1, next_ref[0:HALO, :].astype(F32), 0.0)


def _conv_taps(ext, w_ref, lanes, tm):
    acc = None
    for j in range(CONV_W):
        r0 = HALO - CONV_LEFT + j
        term = ext[r0:r0 + tm, lanes] * w_ref[j:j + 1, lanes]
        acc = term if acc is None else acc + term
    return acc


GDN_HPB = 2


XHALO = 16
GATE_GROUP = 8


def _gate_lane_sources():
    rep = V_HEADS // QK_HEADS
    src = np.zeros((QK_HEADS * GATE_GROUP,), np.int32)
    for dr in range(2):
        for kind in range(2):
            for head in range(V_HEADS):
                hg, vh = divmod(head, rep)
                src[hg * GATE_GROUP + dr * 2 * rep + kind * rep + vh] = dr * 2 * V_HEADS + kind * V_HEADS + head
    return src


def _gdn_in_kernel(x_ref, xp_ref, xn_ref, g_ref, sh_ref, sc_ref, w_ref, wz_ref, wab_ref, cw_ref, alog_ref, dtb_ref,
                   q_ref, k_ref, kt_ref, v_ref, z_ref, cols_ref, rows_ref, h_scr, r_scr, *y_scrs, tm, nw, d):
    i = pl.program_id(1)
    nt = pl.num_programs(1)
    j = pl.program_id(2)
    qk_w = QK_HEADS * HEAD_DIM
    tn = w_ref.shape[1]
    heads_per_tile = tn // HEAD_DIM

    def modulated(x):
        y = x * lax.rsqrt(jnp.mean(x * x, axis=-1, keepdims=True) + EPS) * g_ref[...]
        return (y * (1.0 + sc_ref[...]) + sh_ref[...]).astype(BF16)

    @pl.when(j == 0)
    def _():
        h = modulated(_load_stream_tile(x_ref, nw, d))
        h_scr[0:XHALO, :] = modulated(xp_ref[...])
        h_scr[XHALO:XHALO + tm, :] = h
        h_scr[XHALO + tm:, :] = modulated(xn_ref[...])
        ab = _dot(h, wab_ref[...])
        g = -jnp.exp(alog_ref[...]) * _softplus(ab + dtb_ref[...])
        beta = _sigmoid(ab)
        ri = lax.broadcasted_iota(jnp.int32, (CHUNK, CHUNK), 0)
        ci = lax.broadcasted_iota(jnp.int32, (CHUNK, CHUNK), 1)
        tril = (ri >= ci).astype(F32)
        triu = (ri <= ci).astype(F32)
        lane = lax.broadcasted_iota(jnp.int32, (CHUNK, LANES), 1)
        is_decay = (lane // 2) % 2 == 0
        is_bwd = (lane // 4) % 2 == 1
        for c in range(tm // CHUNK):
            rs = slice(c * CHUNK, (c + 1) * CHUNK)
            pre = jnp.dot(tril, g[rs], preferred_element_type=F32, precision=lax.Precision.HIGHEST)
            suf = jnp.dot(triu, g[rs], preferred_element_type=F32, precision=lax.Precision.HIGHEST)
            gate = jnp.where(is_decay, jnp.where(is_bwd, suf, pre), beta[rs])
            cols_ref[rs, :] = gate
            rows_ref[c] = gate.T

    def project(y_scr):
        y = _dot(h_scr[...], w_ref[...])
        z_ref[...] = _dot(h_scr[XHALO:XHALO + tm, :], wz_ref[...]).astype(BF16)
        y_scr[0:XHALO, :] = jnp.where(i > 0, y[0:XHALO], 0.0)
        y_scr[XHALO:XHALO + tm, :] = y[XHALO:XHALO + tm]
        y_scr[XHALO + tm:, :] = jnp.where(i < nt - 1, y[XHALO + tm:], 0.0)

    def conv_silu(y_scr):
        for h in range(heads_per_tile):
            lanes = slice(h * HEAD_DIM, (h + 1) * HEAD_DIM)
            acc = None
            for tap in range(CONV_W):
                r0 = XHALO - CONV_LEFT + tap
                term = y_scr[r0:r0 + tm, lanes] * cw_ref[tap:tap + 1, lanes]
                acc = term if acc is None else acc + term
            r_scr[:, lanes] = acc * _sigmoid(acc)

    def l2(yh):
        return yh * lax.rsqrt(jnp.sum(yh * yh, axis=-1, keepdims=True) + EPS)

    def finish(tile):
        if tile < qk_w // tn:
            for h in range(heads_per_tile):
                lanes = slice(h * HEAD_DIM, (h + 1) * HEAD_DIM)
                q_ref[:, tile * tn + h * HEAD_DIM:tile * tn + (h + 1) * HEAD_DIM] = (
                    l2(r_scr[:, lanes]) * (HEAD_DIM ** -0.5)).astype(BF16)
        elif tile < 2 * qk_w // tn:
            for h in range(heads_per_tile):
                kn = l2(r_scr[:, h * HEAD_DIM:(h + 1) * HEAD_DIM])
                head = (tile - qk_w // tn) * heads_per_tile + h
                k_ref[:, head * HEAD_DIM:(head + 1) * HEAD_DIM] = kn.astype(BF16)
                for c in range(tm // CHUNK):
                    kc = kn[c * CHUNK:(c + 1) * CHUNK]
                    kt_ref[head, c] = jnp.concatenate([kc, kc], axis=0).T.astype(BF16)
        else:
            vt = tile - 2 * qk_w // tn
            v_ref[:, vt * tn:(vt + 1) * tn] = r_scr[...].astype(BF16)

    n_tiles = (2 * qk_w + V_HEADS * HEAD_DIM) // tn
    for jj in range(n_tiles + 1):
        @pl.when(j == jj)
        def _(jj=jj):
            if jj >= 1:
                conv_silu(y_scrs[(jj - 1) % 2])
            if jj < n_tiles:
                project(y_scrs[jj % 2])
            if jj >= 1:
                finish(jj - 1)


def _gdn_in(x, norm_g, shift, scale, w_main, w_ab, conv_w, a_log, dt_bias, *, col_major):
    b, t, d = x.shape
    qk_w = QK_HEADS * HEAD_DIM
    v_w = V_HEADS * HEAD_DIM
    tn = 1024
    assert w_main.shape[1] == 2 * qk_w + 2 * v_w and qk_w % tn == 0 and v_w % tn == 0
    nj = (2 * qk_w + v_w) // tn
    tz = v_w // nj
    w_qkv, w_z = w_main[:, :2 * qk_w + v_w], w_main[:, 2 * qk_w + v_w:]
    tm, nw, view, xblock, xmap = _stream_tiling(t, d, col_major, 512)
    nt = t // tm
    if col_major:
        rows = t // GRID_W
        assert rows % XHALO == 0
        prev = lambda bi, i, j: (bi, rows // XHALO - 1, jnp.maximum(i * nw - 1, 0))
        nxt = lambda bi, i, j: (bi, 0, jnp.minimum((i + 1) * nw, GRID_W - 1))
    else:
        per = tm // XHALO
        prev = lambda bi, i, j: (bi, jnp.maximum(i * per - 1, 0), 0)
        nxt = lambda bi, i, j: (bi, jnp.minimum((i + 1) * per, t // XHALO - 1), 0)
    src = _gate_lane_sources()
    pad_lanes = lambda a: jnp.pad(a, ((0, 0), (0, LANES - a.shape[1])))
    zeros = jnp.zeros((2, V_HEADS), F32)
    lane_vec = lambda p: pad_lanes(jnp.stack([p, zeros], axis=1).reshape(1, 4 * V_HEADS)[:, src])
    w_gate = pad_lanes(w_ab[:, src])
    vec = pl.BlockSpec((None, 1, d), lambda bi, i, j: (bi, 0, 0))
    const = lambda shape: pl.BlockSpec(shape, lambda bi, i, j: (0,) * len(shape))
    tile = lambda width: pl.BlockSpec((None, tm, width), lambda bi, i, j: (bi, i, 0))
    xv = view(x)
    return pl.pallas_call(
        functools.partial(_gdn_in_kernel, tm=tm, nw=nw, d=d),
        grid=(b, nt, nj + 1),
        in_specs=[pl.BlockSpec(xblock, lambda bi, i, j: xmap(bi, i)),
                  pl.BlockSpec((None, XHALO, d), prev), pl.BlockSpec((None, XHALO, d), nxt),
                  const((1, d)), vec, vec,
                  pl.BlockSpec((d, tn), lambda bi, i, j: (0, jnp.minimum(j, nj - 1))),
                  pl.BlockSpec((d, tz), lambda bi, i, j: (0, jnp.minimum(j, nj - 1))),
                  const((d, LANES)), pl.BlockSpec((CONV_W, tn), lambda bi, i, j: (0, jnp.maximum(j - 1, 0))),
                  const((1, LANES)), const((1, LANES))],
        out_specs=[tile(qk_w), tile(qk_w),
                   pl.BlockSpec((None, QK_HEADS, tm // CHUNK, HEAD_DIM, 2 * CHUNK), lambda bi, i, j: (bi, 0, i, 0, 0)),
                   tile(v_w), pl.BlockSpec((None, tm, tz), lambda bi, i, j: (bi, i, jnp.minimum(j, nj - 1))), tile(LANES),
                   pl.BlockSpec((None, tm // CHUNK, LANES, CHUNK), lambda bi, i, j: (bi, i, 0, 0))],
        out_shape=[jax.ShapeDtypeStruct((b, t, qk_w), BF16), jax.ShapeDtypeStruct((b, t, qk_w), BF16),
                   jax.ShapeDtypeStruct((b, QK_HEADS, t // CHUNK, HEAD_DIM, 2 * CHUNK), BF16),
                   jax.ShapeDtypeStruct((b, t, v_w), BF16), jax.ShapeDtypeStruct((b, t, v_w), BF16),
                   jax.ShapeDtypeStruct((b, t, LANES), F32),
                   jax.ShapeDtypeStruct((b, t // CHUNK, LANES, CHUNK), F32)],
        scratch_shapes=[pltpu.VMEM((tm + 2 * XHALO, d), BF16), pltpu.VMEM((tm, tn), F32),
                        pltpu.VMEM((tm + 2 * XHALO, tn), F32), pltpu.VMEM((tm + 2 * XHALO, tn), F32)],
        compiler_params=_cparams(("parallel", "parallel", "arbitrary")),
        name="gdn_in",
    )(xv, xv, xv, norm_g.reshape(1, d), shift, scale, w_qkv, w_z, w_gate, conv_w, lane_vec(a_log), lane_vec(dt_bias))


INV_BASE = 8


def _block_diag2(pair, first):
    return jnp.concatenate([jnp.where(first, pair, 0.0), jnp.where(first, 0.0, pair)], axis=0).astype(BF16)


SCAN_STAGES = 13
SCAN_SKEW = 2
SCAN_CHUNKS = 16


def _gdn_scan_kernel(qf_ref, kf_ref, tf_ref, vf_ref, cf_ref, rf_ref, qb_ref, kb_ref, tb_ref, vb_ref, cb_ref,
                     rb_ref, s0_ref, of_ref, ob_ref, sout_ref, s_scr, *, cpb, hpb):
    n = pl.program_id(2)

    @pl.when(n == 0)
    def _():
        s_scr[...] = s0_ref[...]

    ri = lax.broadcasted_iota(jnp.int32, (CHUNK, 2 * CHUNK), 0)
    lane = lax.broadcasted_iota(jnp.int32, (CHUNK, 2 * CHUNK), 1)
    first = lane < CHUNK
    first_row = first[0:1]
    ci = jnp.where(first, lane, lane - CHUNK)
    eye = (ri == ci).astype(F32)
    incl = (ri >= ci, ri <= ci)
    strict = (ri > ci, ri < ci)
    same_block = lambda size: (ri // size) == (ci // size)
    refs = ((qf_ref, kf_ref, tf_ref, vf_ref, cf_ref, rf_ref, of_ref),
            (qb_ref, kb_ref, tb_ref, vb_ref, cb_ref, rb_ref, ob_ref))
    pairs = [(hl, d) for hl in range(hpb) for d in range(2)]
    hd = HEAD_DIM
    bd = lambda xs: [_block_diag2(x, first) for x in xs]
    zero_s = jnp.zeros((hd, hd), BF16)
    zero_v = jnp.zeros((CHUNK, hd), BF16)
    state = [s_scr[hl, d] for hl, d in pairs]
    first_gate = pl.program_id(1) * (hpb * GATE_GROUP)

    def chunk_stages(c):
        probs = []
        for hl, d in pairs:
            cc = c if d == 0 else cpb - 1 - c
            probs.append((hl, d, cc, slice(cc * CHUNK, (cc + 1) * CHUNK)))
        idx = range(len(probs))
        gram = [_dot(jnp.concatenate([refs[d][0][sl, hl * hd:(hl + 1) * hd],
                                      refs[d][1][sl, hl * hd:(hl + 1) * hd]], axis=0), refs[d][2][hl, cc])
                for hl, d, cc, sl in probs]
        yield
        lmat, lhs2, qd, scale_u, scale_w, gl = [], [], [], [], [], []
        for j, (hl, d, cc, sl) in enumerate(probs):
            q_ref, k_ref, t_ref, v_ref, c_ref, r_ref, _ = refs[d]
            cols = pltpu.roll(c_ref[sl, :], lax.rem(LANES - first_gate, LANES), 1)
            rows = r_ref[cc, pl.ds(pl.multiple_of(first_gate, GATE_GROUP), hpb * GATE_GROUP), :]
            g0 = hl * GATE_GROUP + d * 4
            pair_row = lambda r: jnp.concatenate([rows[r:r + 1, :], rows[r + 1:r + 2, :]], axis=1)
            grow = pair_row(g0)
            brow = pair_row(g0 + 2)
            gcols = [cols[:, g0 + vh:g0 + vh + 1] for vh in range(2)]
            bcols = [cols[:, g0 + 2 + vh:g0 + 3 + vh] for vh in range(2)]
            gtots = [g[CHUNK - 1:CHUNK, :] if d == 0 else g[0:1, :] for g in gcols]
            gpair = jnp.where(first, gcols[0], gcols[1])
            bpair = jnp.where(first, bcols[0], bcols[1])
            decay = jnp.where(incl[d], jnp.exp(jnp.where(incl[d], gpair - grow, 0.0)), 0.0)
            lmat.append(jnp.where(strict[d], gram[j][CHUNK:] * bpair * decay, 0.0))
            qkm = gram[j][:CHUNK] * decay
            kdt = t_ref[hl, cc].astype(F32) * jnp.exp(jnp.where(first_row, gtots[0], gtots[1]) - grow)
            lhs2.append(jnp.concatenate([qkm, kdt], axis=0).astype(BF16))
            q = q_ref[sl, hl * hd:(hl + 1) * hd].astype(F32)
            qd.append(jnp.concatenate([q * jnp.exp(g) for g in gcols], axis=1))
            scale_u.append(brow)
            scale_w.append(brow * jnp.exp(grow))
            gl.append(jnp.concatenate([jnp.broadcast_to(jnp.exp(g), (1, hd)) for g in gtots], axis=1))
        base = same_block(INV_BASE)
        p = [-jnp.where(base, l, 0.0) for l in lmat]
        p2 = [_dot(p[i].astype(BF16), m) for i, m in enumerate(bd(p))]
        yield
        p2_bd = bd(p2)
        p4 = [_dot(p2[i].astype(BF16), p2_bd[i]) for i in idx]
        tinv = [eye + x for x in p]
        tinv = [tinv[i] + _dot(tinv[i].astype(BF16), p2_bd[i]) for i in idx]
        yield
        tinv = [tinv[i] + _dot(tinv[i].astype(BF16), m) for i, m in enumerate(bd(p4))]
        yield
        size = INV_BASE
        while size < CHUNK:
            off_diag = same_block(2 * size) & jnp.logical_not(same_block(size))
            a16 = [jnp.where(off_diag, l, 0.0).astype(BF16) for l in lmat]
            at = [_dot(a16[i], m) for i, m in enumerate(bd(tinv))]
            yield
            tinv = [tinv[i] - _dot(tinv[i].astype(BF16), m) for i, m in enumerate(bd(at))]
            yield
            size *= 2
        u, w = [], []
        for j, (hl, d, cc, sl) in enumerate(probs):
            k16 = refs[d][1][sl, hl * hd:(hl + 1) * hd]
            v16 = [refs[d][3][sl, (hl * 2 + vh) * hd:(hl * 2 + vh + 1) * hd] for vh in range(2)]
            v_bd16 = jnp.concatenate([jnp.concatenate([v16[0], zero_v], axis=1),
                                      jnp.concatenate([zero_v, v16[1]], axis=1)], axis=0)
            k_bd16 = jnp.concatenate([jnp.concatenate([k16, zero_v], axis=1),
                                      jnp.concatenate([zero_v, k16], axis=1)], axis=0)
            u.append(_dot((tinv[j] * scale_u[j]).astype(BF16), v_bd16))
            w.append(_dot((tinv[j] * scale_w[j]).astype(BF16), k_bd16))
        yield
        wq16 = [jnp.concatenate([w[j], qd[j]], axis=0).astype(BF16) for j in idx]
        s16 = [x.astype(BF16) for x in state]
        s_bd = [jnp.concatenate([jnp.concatenate([x[:, :hd], zero_s], axis=1),
                                 jnp.concatenate([zero_s, x[:, hd:]], axis=1)], axis=0) for x in s16]
        ws = [_dot(wq16[i], s_bd[i]) for i in idx]
        yield
        vnew16 = [(u[i] - ws[i][:CHUNK]).astype(BF16) for i in idx]
        v_bd = [jnp.concatenate([jnp.concatenate([x[:, :hd], zero_v], axis=1),
                                 jnp.concatenate([zero_v, x[:, hd:]], axis=1)], axis=0) for x in vnew16]
        upd = [_dot(lhs2[i], v_bd[i]) for i in idx]
        for i, (hl, d, cc, sl) in enumerate(probs):
            state[i] = state[i] * gl[i] + upd[i][CHUNK:]
            refs[d][6][sl, hl * 2 * hd:(hl + 1) * 2 * hd] = (ws[i][CHUNK:] + upd[i][:CHUNK]).astype(BF16)

    gens = [chunk_stages(c) for c in range(cpb)]
    for tick in range(SCAN_STAGES + SCAN_SKEW * (cpb - 1)):
        for c in range(cpb):
            if 0 <= tick - SCAN_SKEW * c < SCAN_STAGES:
                next(gens[c], None)
    for i, (hl, d) in enumerate(pairs):
        s_scr[hl, d] = state[i]

    @pl.when(n == pl.num_programs(2) - 1)
    def _():
        sout_ref[...] = s_scr[...]


def _gdn_scan(qn, kn, kt, v, cols, rows, s0):
    b, t, _ = qn.shape
    rep = V_HEADS // QK_HEADS
    assert rep == 2
    hpb = GDN_HPB
    blk = _pick(t, SCAN_CHUNKS * CHUNK, CHUNK)
    cpb = blk // CHUNK
    nb = t // blk
    fwd = lambda n: n
    bwd = lambda n: nb - 1 - n

    def specs(order):
        return [pl.BlockSpec((None, blk, hpb * HEAD_DIM), lambda bi, h, n: (bi, order(n), h)),
                pl.BlockSpec((None, blk, hpb * HEAD_DIM), lambda bi, h, n: (bi, order(n), h)),
                pl.BlockSpec((None, hpb, cpb, HEAD_DIM, rep * CHUNK), lambda bi, h, n: (bi, h, order(n), 0, 0)),
                pl.BlockSpec((None, blk, hpb * rep * HEAD_DIM), lambda bi, h, n: (bi, order(n), h)),
                pl.BlockSpec((None, blk, LANES), lambda bi, h, n: (bi, order(n), 0)),
                pl.BlockSpec((None, cpb, LANES, CHUNK), lambda bi, h, n: (bi, order(n), 0, 0))]

    state_block = (None, hpb, 2, HEAD_DIM, rep * HEAD_DIM)
    state_spec = pl.BlockSpec(state_block, lambda bi, h, n: (bi, h, 0, 0, 0))
    o_shape = jax.ShapeDtypeStruct((b, t, V_HEADS * HEAD_DIM), BF16)
    return pl.pallas_call(
        functools.partial(_gdn_scan_kernel, cpb=cpb, hpb=hpb),
        grid=(b, QK_HEADS // hpb, nb),
        in_specs=specs(fwd) + specs(bwd) + [state_spec],
        out_specs=[pl.BlockSpec((None, blk, hpb * rep * HEAD_DIM), lambda bi, h, n: (bi, fwd(n), h)),
                   pl.BlockSpec((None, blk, hpb * rep * HEAD_DIM), lambda bi, h, n: (bi, bwd(n), h)),
                   state_spec],
        out_shape=[o_shape, o_shape, jax.ShapeDtypeStruct(s0.shape, F32)],
        scratch_shapes=[pltpu.VMEM(state_block[1:], F32)],
        compiler_params=_cparams(("parallel", "parallel", "arbitrary")),
        name="gdn_scan",
    )(qn, kn, kt, v, cols, rows, qn, kn, kt, v, cols, rows, s0)


def _lru_scan_kernel(fc_ref, fp_ref, fn_ref, bc_ref, bp_ref, bn_ref, cw_ref, cb_ref, wr_ref, br_ref,
                     wi_ref, bi_ref, lam_ref, h0_ref, hf_ref, hb_ref, hfin_ref, ext, carry, *, tm):
    i = pl.program_id(1)
    nt = pl.num_programs(1)

    @pl.when(i == 0)
    def _():
        carry[...] = h0_ref[...]

    width = cw_ref.shape[1]
    bs = width // LRU_BLOCKS
    nv = tm // SUB
    tpb = bs // LANES
    sub = lax.broadcasted_iota(jnp.int32, (SUB, bs), 0)
    dirs = ((fc_ref, fp_ref, fn_ref, hf_ref, i), (bc_ref, bp_ref, bn_ref, hb_ref, nt - 1 - i))
    for d, (c_ref, p_ref, n_ref, o_ref, tile) in enumerate(dirs):
        for lt in range(width // LANES):
            lt_lanes = slice(lt * LANES, (lt + 1) * LANES)
            ext[lt, 0:HALO, :] = jnp.where(tile > 0, p_ref[HALO_BLK - HALO:, lt_lanes].astype(F32), 0.0)
            ext[lt, HALO:HALO + tm, :] = c_ref[:, lt_lanes].astype(F32)
            ext[lt, HALO + tm:, :] = jnp.where(tile < nt - 1, n_ref[0:HALO, lt_lanes].astype(F32), 0.0)
        for blk in range(LRU_BLOCKS):
            lanes = slice(blk * bs, (blk + 1) * bs)
            groups = []
            for v in range(nv):
                pieces = []
                for lt in range(blk * tpb, (blk + 1) * tpb):
                    acc = None
                    for j in range(CONV_W):
                        term = (ext[lt, pl.ds(HALO - CONV_LEFT + j + v, SUB, stride=nv), :]
                                * cw_ref[j:j + 1, lt * LANES:(lt + 1) * LANES])
                        acc = term if acc is None else acc + term
                    pieces.append(acc)
                groups.append(jnp.concatenate(pieces, axis=1))
            xc = jnp.concatenate(groups, axis=0) + cb_ref[:, lanes]
            xc16 = xc.astype(BF16)
            r = _sigmoid(_dot(xc16, wr_ref[d, blk]) + br_ref[d:d + 1, lanes])
            gi = _sigmoid(_dot(xc16, wi_ref[d, blk]) + bi_ref[d:d + 1, lanes])
            log_a = -RG_C * r * _softplus(-lam_ref[d:d + 1, lanes])
            a = jnp.exp(log_a)
            bv = jnp.sqrt(1.0 - jnp.exp(2.0 * log_a)) * (gi * xc)
            order = range(nv) if d == 0 else range(nv - 1, -1, -1)
            hs, acums = [None] * nv, [None] * nv
            h_run = a_run = None
            for v in order:
                av, bvv = a[v * SUB:(v + 1) * SUB], bv[v * SUB:(v + 1) * SUB]
                h_run = bvv if h_run is None else av * h_run + bvv
                a_run = av if a_run is None else av * a_run
                hs[v], acums[v] = h_run, a_run
            e_end, p_end = h_run, a_run
            s = 1
            while s < SUB:
                shift = s if d == 0 else SUB - s
                keep = (sub >= s) if d == 0 else (sub < SUB - s)
                p_prev = jnp.where(keep, pltpu.roll(p_end, shift, 0), 1.0)
                e_prev = jnp.where(keep, pltpu.roll(e_end, shift, 0), 0.0)
                e_end = p_end * e_prev + e_end
                p_end = p_end * p_prev
                s *= 2
            cin = jnp.broadcast_to(carry[d, 0:1, lanes], (SUB, bs))
            one = 1 if d == 0 else SUB - 1
            first_sub = (sub == 0) if d == 0 else (sub == SUB - 1)
            start = jnp.where(first_sub, cin, pltpu.roll(p_end, one, 0) * cin + pltpu.roll(e_end, one, 0))
            for v in range(nv):
                h = acums[v] * start + hs[v]
                for k in range(tpb):
                    o_ref[blk * tpb + k, pl.ds(v, SUB, stride=nv), :] = h[:, k * LANES:(k + 1) * LANES]
            last = SUB - 1 if d == 0 else 0
            carry[d, 0:1, lanes] = (p_end * cin + e_end)[last:last + 1, :]

    @pl.when(i == nt - 1)
    def _():
        hfin_ref[...] = carry[...]


def _lru_scan(proj, conv_w, conv_b, w_r, b_r, w_i, b_i, lam, h0):
    b, t, two_w = proj.shape
    width = two_w // 2
    tm = _pick(t, 256, CHUNK)
    nt = t // tm
    nlt = width // LANES
    whole = lambda a: pl.BlockSpec(a.shape, lambda bi, i: (0,) * a.ndim)
    state_spec = pl.BlockSpec((None, 2, 1, width), lambda bi, i: (bi, 0, 0, 0))
    consts = [conv_w, conv_b.reshape(1, width), w_r, b_r, w_i, b_i, lam]
    h_shape = jax.ShapeDtypeStruct((b, nlt, t, LANES), F32)
    return pl.pallas_call(
        functools.partial(_lru_scan_kernel, tm=tm),
        grid=(b, nt),
        in_specs=(_halo_specs(tm, t, width, 0, lambda bi, i: (bi, i))
                  + _halo_specs(tm, t, width, 0, lambda bi, i: (bi, nt - 1 - i))
                  + [whole(a) for a in consts] + [state_spec]),
        out_specs=[pl.BlockSpec((None, nlt, tm, LANES), lambda bi, i: (bi, 0, i, 0)),
                   pl.BlockSpec((None, nlt, tm, LANES), lambda bi, i: (bi, 0, nt - 1 - i, 0)),
                   state_spec],
        out_shape=[h_shape, h_shape, jax.ShapeDtypeStruct(h0.shape, F32)],
        scratch_shapes=[pltpu.VMEM((nlt, tm + 2 * HALO, LANES), F32),
                        pltpu.VMEM((2, 1, width), F32)],
        compiler_params=_cparams(("parallel", "arbitrary")),
        name="lru_scan",
    )(proj, proj, proj, proj, proj, proj, *consts, h0)


def _residual_store(xo_ref, x_ref, y, gate, fg_ref, nw, d, rows):
    for k in range(nw):
        lanes = slice(k * d, (k + 1) * d) if nw > 1 else slice(None)
        yk = y[k * rows:(k + 1) * rows] if nw > 1 else y
        xn = x_ref[:, lanes] + gate * yk
        if fg_ref is not None:
            xn = xn * lax.rsqrt(jnp.mean(xn * xn, axis=-1, keepdims=True) + EPS) * fg_ref[...]
        xo_ref[:, lanes] = xn


def _gdn_out_kernel(*refs, nw, d, rows, final):
    of_ref, ob_ref, z_ref, ng_ref, w_ref, x_ref, gate_ref = refs[:7]
    fg_ref = refs[7] if final else None
    xo_ref = refs[-1]
    parts = []
    for h in range(V_HEADS):
        lanes = slice(h * HEAD_DIM, (h + 1) * HEAD_DIM)
        o = of_ref[:, lanes].astype(F32) + ob_ref[:, lanes].astype(F32)
        o = o * lax.rsqrt(jnp.mean(o * o, axis=-1, keepdims=True) + EPS) * ng_ref[...]
        z = z_ref[:, lanes].astype(F32)
        parts.append((o * (z * _sigmoid(z))).astype(BF16))
    y = _dot(jnp.concatenate(parts, axis=1), w_ref[...])
    _residual_store(xo_ref, x_ref, y, gate_ref[...], fg_ref, nw, d, rows)


def _lru_out_kernel(*refs, nw, d, rows, final):
    hf_ref, hb_ref, gt_ref, w_ref, x_ref, gate_ref = refs[:6]
    fg_ref = refs[6] if final else None
    xo_ref = refs[-1]
    gt = gt_ref[...].astype(F32)
    h = jnp.concatenate([hf_ref[lt] + hb_ref[lt] for lt in range(hf_ref.shape[0])], axis=1)
    hg = (h * (gt * _sigmoid(gt))).astype(BF16)
    y = _dot(hg, w_ref[...])
    _residual_store(xo_ref, x_ref, y, gate_ref[...], fg_ref, nw, d, rows)


def _out_proj(kind, branch, gate_src, gate_col, extra, w_out, x, gate, final_g, *, col_major):
    b, t, d = x.shape
    wb = w_out.shape[0]
    tm, nw, view, xblock, xmap = _stream_tiling(t, d, col_major, 512)
    tile = lambda col: pl.BlockSpec((None, tm, wb), lambda bi, i: (bi, i, col))
    if kind == "gdn":
        in_specs = [tile(0), tile(0), tile(gate_col), pl.BlockSpec((1, HEAD_DIM), lambda bi, i: (0, 0))]
        args = [branch[0], branch[1], gate_src, extra.reshape(1, HEAD_DIM)]
        body = _gdn_out_kernel
    else:
        lane_tiles = pl.BlockSpec((None, wb // LANES, tm, LANES), lambda bi, i: (bi, 0, i, 0))
        in_specs = [lane_tiles, lane_tiles, tile(gate_col)]
        args = [branch[0], branch[1], gate_src]
        body = _lru_out_kernel
    in_specs += [pl.BlockSpec(w_out.shape, lambda bi, i: (0, 0)),
                 pl.BlockSpec(xblock, lambda bi, i: xmap(bi, i)),
                 pl.BlockSpec((None, 1, d), lambda bi, i: (bi, 0, 0))]
    args += [w_out, view(x), gate]
    final = final_g is not None
    if final:
        in_specs.append(pl.BlockSpec((1, d), lambda bi, i: (0, 0)))
        args.append(final_g.reshape(1, d))
    xv = view(x)
    out = pl.pallas_call(
        functools.partial(body, nw=nw, d=d, rows=tm // nw, final=final),
        grid=(b, t // tm),
        in_specs=in_specs,
        out_specs=pl.BlockSpec(xblock, lambda bi, i: xmap(bi, i)),
        out_shape=jax.ShapeDtypeStruct(xv.shape, F32),
        compiler_params=_cparams(("parallel", "parallel")),
        name=kind + "_out",
    )(*args)
    return out.reshape(b, t, d)


def _gdn_layer(streams, mods, norm_g, w_in, conv_w, a_log, dt_bias, head_g, w_out, *, col_major, final_g,
               update_ctx):
    qkvz_w = 2 * QK_HEADS * HEAD_DIM + 2 * V_HEADS * HEAD_DIM
    w_main = w_in[:, :qkvz_w].astype(BF16)
    w_ab = jnp.pad(w_in[:, qkvz_w:], ((0, 0), (0, LANES - (w_in.shape[1] - qkvz_w)))).astype(BF16)
    w_out16 = w_out.astype(BF16)
    state = None
    outs = []
    for (x, cm, fg, need_y), (shift, scale, gate) in zip(
            ((streams[0], False, None, update_ctx), (streams[1], col_major, final_g, True)), mods):
        b = x.shape[0]
        if state is None:
            state = jnp.zeros((b, QK_HEADS, 2, HEAD_DIM, V_HEADS // QK_HEADS * HEAD_DIM), F32)
        qn, kn, kt, v, z, cols, rows = _gdn_in(x, norm_g, shift, scale, w_main, w_ab, conv_w, a_log, dt_bias,
                                               col_major=cm)
        o_f, o_b, state = _gdn_scan(qn, kn, kt, v, cols, rows, state)
        if need_y:
            x = _out_proj("gdn", (o_f, o_b), z, 0, head_g, w_out16, x, gate, fg, col_major=cm)
        outs.append(x)
    return outs


def _lru_layer(streams, mods, norm_g, w_in, conv_w, conv_b, w_r, b_r, w_i, b_i, lam, w_out, *, col_major,
               final_g, update_ctx):
    w_in16 = w_in.astype(BF16)
    w_r16 = w_r.astype(BF16)
    w_i16 = w_i.astype(BF16)
    w_out16 = w_out.astype(BF16)
    state = None
    outs = []
    for (x, cm, fg, need_y), (shift, scale, gate) in zip(
            ((streams[0], False, None, update_ctx), (streams[1], col_major, final_g, True)), mods):
        b = x.shape[0]
        if state is None:
            state = jnp.zeros((b, 2, 1, conv_w.shape[1]), F32)
        proj = _in_proj(x, norm_g, shift, scale, w_in16, None, col_major=cm)
        h_f, h_b, state = _lru_scan(proj, conv_w, conv_b, w_r16, b_r, w_i16, b_i, lam, state)
        if need_y:
            x = _out_proj("lru", (h_f, h_b), proj, 1, None, w_out16, x, gate, fg, col_major=cm)
        outs.append(x)
    return outs


def kernel(x, c, ctx, c_ctx, mod_w, mod_b, norm_g, gdn_w_in, gdn_conv, gdn_a_log, gdn_dt_bias, gdn_norm_g,
           gdn_w_out, lru_w_in, lru_conv_w, lru_conv_b, lru_w_r, lru_b_r, lru_w_i, lru_b_i, lru_lambda,
           lru_w_out, final_g):
    b, _, d = x.shape
    depth = mod_w.shape[0]
    assert b + 1 <= MOD_ROWS
    cond = jnp.concatenate([c, c_ctx[None], jnp.zeros((MOD_ROWS - b - 1, d), F32)], axis=0)
    mod = _modulation(cond, mod_w, mod_b)
    for i in range(depth):
        lat = [mod[i, :b, k * d:(k + 1) * d].reshape(b, 1, d) for k in range(3)]
        cmod = [jnp.broadcast_to(mod[i, b, k * d:(k + 1) * d].reshape(1, 1, d), (b, 1, d)) for k in range(3)]
        col_major = (i + i // 2) % 2 == 1
        last = i == depth - 1
        j = i // 2
        common = dict(col_major=col_major, final_g=final_g if last else None, update_ctx=not last)
        if i % 2 == 0:
            ctx, x = _gdn_layer((ctx, x), (cmod, lat), norm_g[i], gdn_w_in[j], gdn_conv[j], gdn_a_log[j],
                                gdn_dt_bias[j], gdn_norm_g[j], gdn_w_out[j], **common)
        else:
            ctx, x = _lru_layer((ctx, x), (cmod, lat), norm_g[i], lru_w_in[j], lru_conv_w[j], lru_conv_b[j],
                                lru_w_r[j], lru_b_r[j], lru_w_i[j], lru_b_i[j], lru_lambda[j], lru_w_out[j],
                                **common)
    return x
```

```python
import functools

import numpy as np
import jax
import jax.numpy as jnp
from jax import lax
from jax.experimental import pallas as pl
from jax.experimental.pallas import tpu as pltpu

F32 = jnp.float32
BF16 = jnp.bfloat16

EPS = 1e-6
GRID_W = 64
HEAD_DIM = 128
QK_HEADS = 8
V_HEADS = 16
CHUNK = 64
CONV_W = 4
CONV_LEFT = 2
HALO = 8
HALO_BLK = 16
LRU_BLOCKS = 4
RG_C = 8.0
LANES = 128
SUB = 8
MOD_ROWS = 8

VMEM_LIMIT = 56 * 1024 * 1024


def _cparams(sem):
    return pltpu.CompilerParams(dimension_semantics=sem, vmem_limit_bytes=VMEM_LIMIT)


def _pick(total, target, mult):
    best = None
    for cand in range(mult, min(total, target) + 1, mult):
        if total % cand == 0:
            best = cand
    assert best is not None, (total, target, mult)
    return best


def _sigmoid(x):
    return 1.0 / (1.0 + jnp.exp(-x))


def _softplus(x):
    return jnp.maximum(x, 0.0) + jnp.log(1.0 + jnp.exp(-jnp.abs(x)))


def _dot(a, b):
    return jnp.dot(a, b, preferred_element_type=F32)


def _mod_kernel(c_ref, w_ref, b_ref, o_ref):
    c = c_ref[...]
    sc = c * _sigmoid(c)
    o_ref[...] = jnp.dot(sc, w_ref[...], preferred_element_type=F32,
                         precision=lax.Precision.HIGHEST) + b_ref[...]


def _modulation(cond, mod_w, mod_b):
    depth, d, n3 = mod_w.shape
    return pl.pallas_call(
        _mod_kernel,
        grid=(depth, n3 // d),
        in_specs=[pl.BlockSpec((MOD_ROWS, d), lambda i, j: (0, 0)),
                  pl.BlockSpec((None, d, d), lambda i, j: (i, 0, j)),
                  pl.BlockSpec((None, 1, d), lambda i, j: (i, 0, j))],
        out_specs=pl.BlockSpec((None, MOD_ROWS, d), lambda i, j: (i, 0, j)),
        out_shape=jax.ShapeDtypeStruct((depth, MOD_ROWS, n3), F32),
        compiler_params=_cparams(("parallel", "parallel")),
        name="modulation",
    )(cond, mod_w, mod_b.reshape(depth, 1, n3))


def _stream_tiling(t, d, col_major, target):
    if not col_major:
        tm = _pick(t, target, CHUNK)
        return tm, 1, (lambda a: a), (None, tm, d), (lambda b, i: (b, i, 0))
    rows = t // GRID_W
    assert rows * GRID_W == t and rows % 8 == 0
    nw = _pick(GRID_W, max(target // rows, 1), 1)
    tm = rows * nw
    return (tm, nw, (lambda a: a.reshape(a.shape[0], rows, GRID_W * d)),
            (None, rows, nw * d), (lambda b, i: (b, 0, i)))


def _load_stream_tile(x_ref, nw, d):
    if nw == 1:
        return x_ref[...]
    return jnp.concatenate([x_ref[:, k * d:(k + 1) * d] for k in range(nw)], axis=0)


def _in_proj_kernel(*refs, nw, d, has_extra):
    if has_extra:
        x_ref, g_ref, sh_ref, sc_ref, w_ref, we_ref, o_ref, oe_ref, h_scr = refs
    else:
        x_ref, g_ref, sh_ref, sc_ref, w_ref, o_ref, h_scr = refs

    @pl.when(pl.program_id(2) == 0)
    def _():
        x = _load_stream_tile(x_ref, nw, d)
        y = x * lax.rsqrt(jnp.mean(x * x, axis=-1, keepdims=True) + EPS) * g_ref[...]
        h = (y * (1.0 + sc_ref[...]) + sh_ref[...]).astype(BF16)
        h_scr[...] = h
        if has_extra:
            oe_ref[...] = _dot(h, we_ref[...])

    o_ref[...] = _dot(h_scr[...], w_ref[...]).astype(BF16)


def _in_proj(x, norm_g, shift, scale, w, w_extra, *, col_major):
    b, t, d = x.shape
    n = w.shape[1]
    tn = _pick(n, 1024, LANES)
    tm, nw, view, xblock, xmap = _stream_tiling(t, d, col_major, 1024)
    has_extra = w_extra is not None
    vec = pl.BlockSpec((None, 1, d), lambda bi, i, j: (bi, 0, 0))
    in_specs = [pl.BlockSpec(xblock, lambda bi, i, j: xmap(bi, i)),
                pl.BlockSpec((1, d), lambda bi, i, j: (0, 0)), vec, vec,
                pl.BlockSpec((d, tn), lambda bi, i, j: (0, j))]
    out_specs = [pl.BlockSpec((None, tm, tn), lambda bi, i, j: (bi, i, j))]
    out_shape = [jax.ShapeDtypeStruct((b, t, n), BF16)]
    args = [view(x), norm_g.reshape(1, d), shift, scale, w]
    if has_extra:
        in_specs.append(pl.BlockSpec((d, LANES), lambda bi, i, j: (0, 0)))
        out_specs.append(pl.BlockSpec((None, tm, LANES), lambda bi, i, j: (bi, i, 0)))
        out_shape.append(jax.ShapeDtypeStruct((b, t, LANES), F32))
        args.append(w_extra)
    outs = pl.pallas_call(
        functools.partial(_in_proj_kernel, nw=nw, d=d, has_extra=has_extra),
        grid=(b, t // tm, n // tn),
        in_specs=in_specs, out_specs=out_specs, out_shape=out_shape,
        scratch_shapes=[pltpu.VMEM((tm, d), BF16)],
        compiler_params=_cparams(("parallel", "parallel", "arbitrary")),
        name="in_proj",
    )(*args)
    return outs if has_extra else outs[0]


def _halo_specs(tm, t, width, col_block, tile_of):
    per = tm // HALO_BLK
    last = t // HALO_BLK - 1

    def cur(*g):
        bi, i = tile_of(*g)
        return (bi, i, col_block)

    def prev(*g):
        bi, i = tile_of(*g)
        return (bi, jnp.maximum(i * per - 1, 0), col_block)

    def nxt(*g):
        bi, i = tile_of(*g)
        return (bi, jnp.minimum((i + 1) * per, last), col_block)

    return [pl.BlockSpec((None, tm, width), cur),
            pl.BlockSpec((None, HALO_BLK, width), prev),
            pl.BlockSpec((None, HALO_BLK, width), nxt)]


GDN_HPB = 2


XHALO = 16
GATE_GROUP = 8


def _gate_lane_sources():
    rep = V_HEADS // QK_HEADS
    src = np.zeros((QK_HEADS * GATE_GROUP,), np.int32)
    for dr in range(2):
        for kind in range(2):
            for head in range(V_HEADS):
                hg, vh = divmod(head, rep)
                src[hg * GATE_GROUP + dr * 2 * rep + kind * rep + vh] = dr * 2 * V_HEADS + kind * V_HEADS + head
    return src


def _gdn_in_kernel(x_ref, xp_ref, xn_ref, g_ref, sh_ref, sc_ref, w_ref, wz_ref, wab_ref, cw_ref, alog_ref, dtb_ref,
                   q_ref, k_ref, kt_ref, v_ref, z_ref, cols_ref, rows_ref, h_scr, r_scr, *y_scrs, tm, nw, d):
    i = pl.program_id(1)
    nt = pl.num_programs(1)
    j = pl.program_id(2)
    qk_w = QK_HEADS * HEAD_DIM
    tn = w_ref.shape[1]
    heads_per_tile = tn // HEAD_DIM

    def modulated(x):
        y = x * lax.rsqrt(jnp.mean(x * x, axis=-1, keepdims=True) + EPS) * g_ref[...]
        return (y * (1.0 + sc_ref[...]) + sh_ref[...]).astype(BF16)

    @pl.when(j == 0)
    def _():
        h = modulated(_load_stream_tile(x_ref, nw, d))
        h_scr[0:XHALO, :] = modulated(xp_ref[...])
        h_scr[XHALO:XHALO + tm, :] = h
        h_scr[XHALO + tm:, :] = modulated(xn_ref[...])
        ab = _dot(h, wab_ref[...])
        g = -jnp.exp(alog_ref[...]) * _softplus(ab + dtb_ref[...])
        beta = _sigmoid(ab)
        ri = lax.broadcasted_iota(jnp.int32, (CHUNK, CHUNK), 0)
        ci = lax.broadcasted_iota(jnp.int32, (CHUNK, CHUNK), 1)
        tril = (ri >= ci).astype(F32)
        triu = (ri <= ci).astype(F32)
        lane = lax.broadcasted_iota(jnp.int32, (CHUNK, LANES), 1)
        is_decay = (lane // 2) % 2 == 0
        is_bwd = (lane // 4) % 2 == 1
        for c in range(tm // CHUNK):
            rs = slice(c * CHUNK, (c + 1) * CHUNK)
            pre = jnp.dot(tril, g[rs], preferred_element_type=F32, precision=lax.Precision.HIGHEST)
            suf = jnp.dot(triu, g[rs], preferred_element_type=F32, precision=lax.Precision.HIGHEST)
            gate = jnp.where(is_decay, jnp.where(is_bwd, suf, pre), beta[rs])
            cols_ref[rs, :] = gate
            rows_ref[c] = gate.T

    def project(y_scr):
        y = _dot(h_scr[...], w_ref[...])
        z_ref[...] = _dot(h_scr[XHALO:XHALO + tm, :], wz_ref[...]).astype(BF16)
        y_scr[0:XHALO, :] = jnp.where(i > 0, y[0:XHALO], 0.0)
        y_scr[XHALO:XHALO + tm, :] = y[XHALO:XHALO + tm]
        y_scr[XHALO + tm:, :] = jnp.where(i < nt - 1, y[XHALO + tm:], 0.0)

    def conv_silu(y_scr):
        for h in range(heads_per_tile):
            lanes = slice(h * HEAD_DIM, (h + 1) * HEAD_DIM)
            acc = None
            for tap in range(CONV_W):
                r0 = XHALO - CONV_LEFT + tap
                term = y_scr[r0:r0 + tm, lanes] * cw_ref[tap:tap + 1, lanes]
                acc = term if acc is None else acc + term
            r_scr[:, lanes] = acc * _sigmoid(acc)

    def l2(yh):
        return yh * lax.rsqrt(jnp.sum(yh * yh, axis=-1, keepdims=True) + EPS)

    def finish(tile):
        if tile < qk_w // tn:
            for h in range(heads_per_tile):
                lanes = slice(h * HEAD_DIM, (h + 1) * HEAD_DIM)
                q_ref[:, tile * tn + h * HEAD_DIM:tile * tn + (h + 1) * HEAD_DIM] = (
                    l2(r_scr[:, lanes]) * (HEAD_DIM ** -0.5)).astype(BF16)
        elif tile < 2 * qk_w // tn:
            for h in range(heads_per_tile):
                kn = l2(r_scr[:, h * HEAD_DIM:(h + 1) * HEAD_DIM])
                head = (tile - qk_w // tn) * heads_per_tile + h
                k_ref[:, head * HEAD_DIM:(head + 1) * HEAD_DIM] = kn.astype(BF16)
                for c in range(tm // CHUNK):
                    kc = kn[c * CHUNK:(c + 1) * CHUNK]
                    kt_ref[head, c] = jnp.concatenate([kc, kc], axis=0).T.astype(BF16)
        else:
            vt = tile - 2 * qk_w // tn
            v_ref[:, vt * tn:(vt + 1) * tn] = r_scr[...].astype(BF16)

    n_tiles = (2 * qk_w + V_HEADS * HEAD_DIM) // tn
    for jj in range(n_tiles + 1):
        @pl.when(j == jj)
        def _(jj=jj):
            if jj >= 1:
                conv_silu(y_scrs[(jj - 1) % 2])
            if jj < n_tiles:
                project(y_scrs[jj % 2])
            if jj >= 1:
                finish(jj - 1)


def _gdn_in(x, norm_g, shift, scale, w_main, w_ab, conv_w, a_log, dt_bias, *, col_major):
    b, t, d = x.shape
    qk_w = QK_HEADS * HEAD_DIM
    v_w = V_HEADS * HEAD_DIM
    tn = 1024
    assert w_main.shape[1] == 2 * qk_w + 2 * v_w and qk_w % tn == 0 and v_w % tn == 0
    nj = (2 * qk_w + v_w) // tn
    tz = v_w // nj
    w_qkv, w_z = w_main[:, :2 * qk_w + v_w], w_main[:, 2 * qk_w + v_w:]
    tm, nw, view, xblock, xmap = _stream_tiling(t, d, col_major, 512)
    nt = t // tm
    if col_major:
        rows = t // GRID_W
        assert rows % XHALO == 0
        prev = lambda bi, i, j: (bi, rows // XHALO - 1, jnp.maximum(i * nw - 1, 0))
        nxt = lambda bi, i, j: (bi, 0, jnp.minimum((i + 1) * nw, GRID_W - 1))
    else:
        per = tm // XHALO
        prev = lambda bi, i, j: (bi, jnp.maximum(i * per - 1, 0), 0)
        nxt = lambda bi, i, j: (bi, jnp.minimum((i + 1) * per, t // XHALO - 1), 0)
    src = _gate_lane_sources()
    pad_lanes = lambda a: jnp.pad(a, ((0, 0), (0, LANES - a.shape[1])))
    zeros = jnp.zeros((2, V_HEADS), F32)
    lane_vec = lambda p: pad_lanes(jnp.stack([p, zeros], axis=1).reshape(1, 4 * V_HEADS)[:, src])
    w_gate = pad_lanes(w_ab[:, src])
    vec = pl.BlockSpec((None, 1, d), lambda bi, i, j: (bi, 0, 0))
    const = lambda shape: pl.BlockSpec(shape, lambda bi, i, j: (0,) * len(shape))
    tile = lambda width: pl.BlockSpec((None, tm, width), lambda bi, i, j: (bi, i, 0))
    xv = view(x)
    return pl.pallas_call(
        functools.partial(_gdn_in_kernel, tm=tm, nw=nw, d=d),
        grid=(b, nt, nj + 1),
        in_specs=[pl.BlockSpec(xblock, lambda bi, i, j: xmap(bi, i)),
                  pl.BlockSpec((None, XHALO, d), prev), pl.BlockSpec((None, XHALO, d), nxt),
                  const((1, d)), vec, vec,
                  pl.BlockSpec((d, tn), lambda bi, i, j: (0, jnp.minimum(j, nj - 1))),
                  pl.BlockSpec((d, tz), lambda bi, i, j: (0, jnp.minimum(j, nj - 1))),
                  const((d, LANES)), pl.BlockSpec((CONV_W, tn), lambda bi, i, j: (0, jnp.maximum(j - 1, 0))),
                  const((1, LANES)), const((1, LANES))],
        out_specs=[tile(qk_w), tile(qk_w),
                   pl.BlockSpec((None, QK_HEADS, tm // CHUNK, HEAD_DIM, 2 * CHUNK), lambda bi, i, j: (bi, 0, i, 0, 0)),
                   tile(v_w), pl.BlockSpec((None, tm, tz), lambda bi, i, j: (bi, i, jnp.minimum(j, nj - 1))), tile(LANES),
                   pl.BlockSpec((None, tm // CHUNK, LANES, CHUNK), lambda bi, i, j: (bi, i, 0, 0))],
        out_shape=[jax.ShapeDtypeStruct((b, t, qk_w), BF16), jax.ShapeDtypeStruct((b, t, qk_w), BF16),
                   jax.ShapeDtypeStruct((b, QK_HEADS, t // CHUNK, HEAD_DIM, 2 * CHUNK), BF16),
                   jax.ShapeDtypeStruct((b, t, v_w), BF16), jax.ShapeDtypeStruct((b, t, v_w), BF16),
                   jax.ShapeDtypeStruct((b, t, LANES), F32),
                   jax.ShapeDtypeStruct((b, t // CHUNK, LANES, CHUNK), F32)],
        scratch_shapes=[pltpu.VMEM((tm + 2 * XHALO, d), BF16), pltpu.VMEM((tm, tn), F32),
                        pltpu.VMEM((tm + 2 * XHALO, tn), F32), pltpu.VMEM((tm + 2 * XHALO, tn), F32)],
        compiler_params=_cparams(("parallel", "parallel", "arbitrary")),
        name="gdn_in",
    )(xv, xv, xv, norm_g.reshape(1, d), shift, scale, w_qkv, w_z, w_gate, conv_w, lane_vec(a_log), lane_vec(dt_bias))


INV_BASE = 8


def _block_diag2(pair, first):
    return jnp.concatenate([jnp.where(first, pair, 0.0), jnp.where(first, 0.0, pair)], axis=0).astype(BF16)


SCAN_STAGES = 13
SCAN_SKEW = 2
SCAN_CHUNKS = 16


def _gdn_scan_kernel(qf_ref, kf_ref, tf_ref, vf_ref, cf_ref, rf_ref, qb_ref, kb_ref, tb_ref, vb_ref, cb_ref,
                     rb_ref, s0_ref, of_ref, ob_ref, sout_ref, s_scr, *, cpb, hpb):
    n = pl.program_id(2)

    @pl.when(n == 0)
    def _():
        s_scr[...] = s0_ref[...]

    ri = lax.broadcasted_iota(jnp.int32, (CHUNK, 2 * CHUNK), 0)
    lane = lax.broadcasted_iota(jnp.int32, (CHUNK, 2 * CHUNK), 1)
    first = lane < CHUNK
    first_row = first[0:1]
    ci = jnp.where(first, lane, lane - CHUNK)
    eye = (ri == ci).astype(F32)
    incl = (ri >= ci, ri <= ci)
    strict = (ri > ci, ri < ci)
    same_block = lambda size: (ri // size) == (ci // size)
    refs = ((qf_ref, kf_ref, tf_ref, vf_ref, cf_ref, rf_ref, of_ref),
            (qb_ref, kb_ref, tb_ref, vb_ref, cb_ref, rb_ref, ob_ref))
    pairs = [(hl, d) for hl in range(hpb) for d in range(2)]
    hd = HEAD_DIM
    bd = lambda xs: [_block_diag2(x, first) for x in xs]
    zero_s = jnp.zeros((hd, hd), BF16)
    zero_v = jnp.zeros((CHUNK, hd), BF16)
    state = [s_scr[hl, d] for hl, d in pairs]
    first_gate = pl.program_id(1) * (hpb * GATE_GROUP)

    def chunk_stages(c):
        probs = []
        for hl, d in pairs:
            cc = c if d == 0 else cpb - 1 - c
            probs.append((hl, d, cc, slice(cc * CHUNK, (cc + 1) * CHUNK)))
        idx = range(len(probs))
        gram = [_dot(jnp.concatenate([refs[d][0][sl, hl * hd:(hl + 1) * hd],
                                      refs[d][1][sl, hl * hd:(hl + 1) * hd]], axis=0), refs[d][2][hl, cc])
                for hl, d, cc, sl in probs]
        yield
        lmat, lhs2, qd, scale_u, scale_w, gl = [], [], [], [], [], []
        for j, (hl, d, cc, sl) in enumerate(probs):
            q_ref, k_ref, t_ref, v_ref, c_ref, r_ref, _ = refs[d]
            cols = pltpu.roll(c_ref[sl, :], lax.rem(LANES - first_gate, LANES), 1)
            rows = r_ref[cc, pl.ds(pl.multiple_of(first_gate, GATE_GROUP), hpb * GATE_GROUP), :]
            g0 = hl * GATE_GROUP + d * 4
            pair_row = lambda r: jnp.concatenate([rows[r:r + 1, :], rows[r + 1:r + 2, :]], axis=1)
            grow = pair_row(g0)
            brow = pair_row(g0 + 2)
            gcols = [cols[:, g0 + vh:g0 + vh + 1] for vh in range(2)]
            bcols = [cols[:, g0 + 2 + vh:g0 + 3 + vh] for vh in range(2)]
            gtots = [g[CHUNK - 1:CHUNK, :] if d == 0 else g[0:1, :] for g in gcols]
            gpair = jnp.where(first, gcols[0], gcols[1])
            bpair = jnp.where(first, bcols[0], bcols[1])
            decay = jnp.where(incl[d], jnp.exp(jnp.where(incl[d], gpair - grow, 0.0)), 0.0)
            lmat.append(jnp.where(strict[d], gram[j][CHUNK:] * bpair * decay, 0.0))
            qkm = gram[j][:CHUNK] * decay
            kdt = t_ref[hl, cc].astype(F32) * jnp.exp(jnp.where(first_row, gtots[0], gtots[1]) - grow)
            lhs2.append(jnp.concatenate([qkm, kdt], axis=0).astype(BF16))
            q = q_ref[sl, hl * hd:(hl + 1) * hd].astype(F32)
            qd.append(jnp.concatenate([q * jnp.exp(g) for g in gcols], axis=1))
            scale_u.append(brow)
            scale_w.append(brow * jnp.exp(grow))
            gl.append(jnp.concatenate([jnp.broadcast_to(jnp.exp(g), (1, hd)) for g in gtots], axis=1))
        base = same_block(INV_BASE)
        p = [-jnp.where(base, l, 0.0) for l in lmat]
        p2 = [_dot(p[i].astype(BF16), m) for i, m in enumerate(bd(p))]
        yield
        p2_bd = bd(p2)
        p4 = [_dot(p2[i].astype(BF16), p2_bd[i]) for i in idx]
        tinv = [eye + x for x in p]
        tinv = [tinv[i] + _dot(tinv[i].astype(BF16), p2_bd[i]) for i in idx]
        yield
        tinv = [tinv[i] + _dot(tinv[i].astype(BF16), m) for i, m in enumerate(bd(p4))]
        yield
        size = INV_BASE
        while size < CHUNK:
            off_diag = same_block(2 * size) & jnp.logical_not(same_block(size))
            a16 = [jnp.where(off_diag, l, 0.0).astype(BF16) for l in lmat]
            at = [_dot(a16[i], m) for i, m in enumerate(bd(tinv))]
            yield
            tinv = [tinv[i] - _dot(tinv[i].astype(BF16), m) for i, m in enumerate(bd(at))]
            yield
            size *= 2
        u, w = [], []
        for j, (hl, d, cc, sl) in enumerate(probs):
            k16 = refs[d][1][sl, hl * hd:(hl + 1) * hd]
            v16 = [refs[d][3][sl, (hl * 2 + vh) * hd:(hl * 2 + vh + 1) * hd] for vh in range(2)]
            v_bd16 = jnp.concatenate([jnp.concatenate([v16[0], zero_v], axis=1),
                                      jnp.concatenate([zero_v, v16[1]], axis=1)], axis=0)
            k_bd16 = jnp.concatenate([jnp.concatenate([k16, zero_v], axis=1),
                                      jnp.concatenate([zero_v, k16], axis=1)], axis=0)
            u.append(_dot((tinv[j] * scale_u[j]).astype(BF16), v_bd16))
            w.append(_dot((tinv[j] * scale_w[j]).astype(BF16), k_bd16))
        yield
        wq16 = [jnp.concatenate([w[j], qd[j]], axis=0).astype(BF16) for j in idx]
        s16 = [x.astype(BF16) for x in state]
        s_bd = [jnp.concatenate([jnp.concatenate([x[:, :hd], zero_s], axis=1),
                                 jnp.concatenate([zero_s, x[:, hd:]], axis=1)], axis=0) for x in s16]
        ws = [_dot(wq16[i], s_bd[i]) for i in idx]
        yield
        vnew16 = [(u[i] - ws[i][:CHUNK]).astype(BF16) for i in idx]
        v_bd = [jnp.concatenate([jnp.concatenate([x[:, :hd], zero_v], axis=1),
                                 jnp.concatenate([zero_v, x[:, hd:]], axis=1)], axis=0) for x in vnew16]
        upd = [_dot(lhs2[i], v_bd[i]) for i in idx]
        for i, (hl, d, cc, sl) in enumerate(probs):
            state[i] = state[i] * gl[i] + upd[i][CHUNK:]
            refs[d][6][sl, hl * 2 * hd:(hl + 1) * 2 * hd] = (ws[i][CHUNK:] + upd[i][:CHUNK]).astype(BF16)

    gens = [chunk_stages(c) for c in range(cpb)]
    for tick in range(SCAN_STAGES + SCAN_SKEW * (cpb - 1)):
        for c in range(cpb):
            if 0 <= tick - SCAN_SKEW * c < SCAN_STAGES:
                next(gens[c], None)
    for i, (hl, d) in enumerate(pairs):
        s_scr[hl, d] = state[i]

    @pl.when(n == pl.num_programs(2) - 1)
    def _():
        sout_ref[...] = s_scr[...]


def _gdn_scan(qn, kn, kt, v, cols, rows, s0):
    b, t, _ = qn.shape
    rep = V_HEADS // QK_HEADS
    assert rep == 2
    hpb = GDN_HPB
    blk = _pick(t, SCAN_CHUNKS * CHUNK, CHUNK)
    cpb = blk // CHUNK
    nb = t // blk
    fwd = lambda n: n
    bwd = lambda n: nb - 1 - n

    def specs(order):
        return [pl.BlockSpec((None, blk, hpb * HEAD_DIM), lambda bi, h, n: (bi, order(n), h)),
                pl.BlockSpec((None, blk, hpb * HEAD_DIM), lambda bi, h, n: (bi, order(n), h)),
                pl.BlockSpec((None, hpb, cpb, HEAD_DIM, rep * CHUNK), lambda bi, h, n: (bi, h, order(n), 0, 0)),
                pl.BlockSpec((None, blk, hpb * rep * HEAD_DIM), lambda bi, h, n: (bi, order(n), h)),
                pl.BlockSpec((None, blk, LANES), lambda bi, h, n: (bi, order(n), 0)),
                pl.BlockSpec((None, cpb, LANES, CHUNK), lambda bi, h, n: (bi, order(n), 0, 0))]

    state_block = (None, hpb, 2, HEAD_DIM, rep * HEAD_DIM)
    state_spec = pl.BlockSpec(state_block, lambda bi, h, n: (bi, h, 0, 0, 0))
    o_shape = jax.ShapeDtypeStruct((b, t, V_HEADS * HEAD_DIM), BF16)
    return pl.pallas_call(
        functools.partial(_gdn_scan_kernel, cpb=cpb, hpb=hpb),
        grid=(b, QK_HEADS // hpb, nb),
        in_specs=specs(fwd) + specs(bwd) + [state_spec],
        out_specs=[pl.BlockSpec((None, blk, hpb * rep * HEAD_DIM), lambda bi, h, n: (bi, fwd(n), h)),
                   pl.BlockSpec((None, blk, hpb * rep * HEAD_DIM), lambda bi, h, n: (bi, bwd(n), h)),
                   state_spec],
        out_shape=[o_shape, o_shape, jax.ShapeDtypeStruct(s0.shape, F32)],
        scratch_shapes=[pltpu.VMEM(state_block[1:], F32)],
        compiler_params=_cparams(("parallel", "parallel", "arbitrary")),
        name="gdn_scan",
    )(qn, kn, kt, v, cols, rows, qn, kn, kt, v, cols, rows, s0)


def _lru_scan_kernel(fc_ref, fp_ref, fn_ref, bc_ref, bp_ref, bn_ref, cw_ref, cb_ref, wr_ref, br_ref,
                     wi_ref, bi_ref, lam_ref, h0_ref, hf_ref, hb_ref, hfin_ref, ext, carry, *, tm):
    i = pl.program_id(1)
    nt = pl.num_programs(1)

    @pl.when(i == 0)
    def _():
        carry[...] = h0_ref[...]

    width = cw_ref.shape[1]
    bs = width // LRU_BLOCKS
    nv = tm // SUB
    tpb = bs // LANES
    sub = lax.broadcasted_iota(jnp.int32, (SUB, bs), 0)
    dirs = ((fc_ref, fp_ref, fn_ref, hf_ref, i), (bc_ref, bp_ref, bn_ref, hb_ref, nt - 1 - i))
    for d, (c_ref, p_ref, n_ref, o_ref, tile) in enumerate(dirs):
        for lt in range(width // LANES):
            lt_lanes = slice(lt * LANES, (lt + 1) * LANES)
            ext[lt, 0:HALO, :] = jnp.where(tile > 0, p_ref[HALO_BLK - HALO:, lt_lanes].astype(F32), 0.0)
            ext[lt, HALO:HALO + tm, :] = c_ref[:, lt_lanes].astype(F32)
            ext[lt, HALO + tm:, :] = jnp.where(tile < nt - 1, n_ref[0:HALO, lt_lanes].astype(F32), 0.0)
        for blk in range(LRU_BLOCKS):
            lanes = slice(blk * bs, (blk + 1) * bs)
            groups = []
            for v in range(nv):
                pieces = []
                for lt in range(blk * tpb, (blk + 1) * tpb):
                    acc = None
                    for j in range(CONV_W):
                        term = (ext[lt, pl.ds(HALO - CONV_LEFT + j + v, SUB, stride=nv), :]
                                * cw_ref[j:j + 1, lt * LANES:(lt + 1) * LANES])
                        acc = term if acc is None else acc + term
                    pieces.append(acc)
                groups.append(jnp.concatenate(pieces, axis=1))
            xc = jnp.concatenate(groups, axis=0) + cb_ref[:, lanes]
            xc16 = xc.astype(BF16)
            r = _sigmoid(_dot(xc16, wr_ref[d, blk]) + br_ref[d:d + 1, lanes])
            gi = _sigmoid(_dot(xc16, wi_ref[d, blk]) + bi_ref[d:d + 1, lanes])
            log_a = -RG_C * r * _softplus(-lam_ref[d:d + 1, lanes])
            a = jnp.exp(log_a)
            bv = jnp.sqrt(1.0 - a * a) * (gi * xc)
            order = range(nv) if d == 0 else range(nv - 1, -1, -1)
            hs, acums = [None] * nv, [None] * nv
            h_run = a_run = None
            for v in order:
                av, bvv = a[v * SUB:(v + 1) * SUB], bv[v * SUB:(v + 1) * SUB]
                h_run = bvv if h_run is None else av * h_run + bvv
                a_run = av if a_run is None else av * a_run
                hs[v], acums[v] = h_run, a_run
            e_end, p_end = h_run, a_run
            s = 1
            while s < SUB:
                shift = s if d == 0 else SUB - s
                keep = (sub >= s) if d == 0 else (sub < SUB - s)
                p_prev = jnp.where(keep, pltpu.roll(p_end, shift, 0), 1.0)
                e_prev = jnp.where(keep, pltpu.roll(e_end, shift, 0), 0.0)
                e_end = p_end * e_prev + e_end
                p_end = p_end * p_prev
                s *= 2
            cin = jnp.broadcast_to(carry[d, 0:1, lanes], (SUB, bs))
            one = 1 if d == 0 else SUB - 1
            first_sub = (sub == 0) if d == 0 else (sub == SUB - 1)
            start = jnp.where(first_sub, cin, pltpu.roll(p_end, one, 0) * cin + pltpu.roll(e_end, one, 0))
            for v in range(nv):
                h = acums[v] * start + hs[v]
                for k in range(tpb):
                    o_ref[blk * tpb + k, pl.ds(v, SUB, stride=nv), :] = h[:, k * LANES:(k + 1) * LANES]
            last = SUB - 1 if d == 0 else 0
            carry[d, 0:1, lanes] = (p_end * cin + e_end)[last:last + 1, :]

    @pl.when(i == nt - 1)
    def _():
        hfin_ref[...] = carry[...]


def _lru_scan(proj, conv_w, conv_b, w_r, b_r, w_i, b_i, lam, h0):
    b, t, two_w = proj.shape
    width = two_w // 2
    tm = _pick(t, 256, CHUNK)
    nt = t // tm
    nlt = width // LANES
    whole = lambda a: pl.BlockSpec(a.shape, lambda bi, i: (0,) * a.ndim)
    state_spec = pl.BlockSpec((None, 2, 1, width), lambda bi, i: (bi, 0, 0, 0))
    consts = [conv_w, conv_b.reshape(1, width), w_r, b_r, w_i, b_i, lam]
    h_shape = jax.ShapeDtypeStruct((b, nlt, t, LANES), F32)
    return pl.pallas_call(
        functools.partial(_lru_scan_kernel, tm=tm),
        grid=(b, nt),
        in_specs=(_halo_specs(tm, t, width, 0, lambda bi, i: (bi, i))
                  + _halo_specs(tm, t, width, 0, lambda bi, i: (bi, nt - 1 - i))
                  + [whole(a) for a in consts] + [state_spec]),
        out_specs=[pl.BlockSpec((None, nlt, tm, LANES), lambda bi, i: (bi, 0, i, 0)),
                   pl.BlockSpec((None, nlt, tm, LANES), lambda bi, i: (bi, 0, nt - 1 - i, 0)),
                   state_spec],
        out_shape=[h_shape, h_shape, jax.ShapeDtypeStruct(h0.shape, F32)],
        scratch_shapes=[pltpu.VMEM((nlt, tm + 2 * HALO, LANES), F32),
                        pltpu.VMEM((2, 1, width), F32)],
        compiler_params=_cparams(("parallel", "arbitrary")),
        name="lru_scan",
    )(proj, proj, proj, proj, proj, proj, *consts, h0)


def _residual_store(xo_ref, x_ref, y, gate, fg_ref, nw, d, rows):
    for k in range(nw):
        lanes = slice(k * d, (k + 1) * d) if nw > 1 else slice(None)
        yk = y[k * rows:(k + 1) * rows] if nw > 1 else y
        xn = x_ref[:, lanes] + gate * yk
        if fg_ref is not None:
            xn = xn * lax.rsqrt(jnp.mean(xn * xn, axis=-1, keepdims=True) + EPS) * fg_ref[...]
        xo_ref[:, lanes] = xn


def _gdn_out_kernel(*refs, nw, d, rows, final):
    of_ref, ob_ref, z_ref, ng_ref, w_ref, x_ref, gate_ref = refs[:7]
    fg_ref = refs[7] if final else None
    xo_ref = refs[-1]
    parts = []
    for h in range(V_HEADS):
        lanes = slice(h * HEAD_DIM, (h + 1) * HEAD_DIM)
        o = of_ref[:, lanes].astype(F32) + ob_ref[:, lanes].astype(F32)
        o = o * lax.rsqrt(jnp.mean(o * o, axis=-1, keepdims=True) + EPS) * ng_ref[...]
        z = z_ref[:, lanes].astype(F32)
        parts.append((o * (z * _sigmoid(z))).astype(BF16))
    y = _dot(jnp.concatenate(parts, axis=1), w_ref[...])
    _residual_store(xo_ref, x_ref, y, gate_ref[...], fg_ref, nw, d, rows)


def _lru_out_kernel(*refs, nw, d, rows, final):
    hf_ref, hb_ref, gt_ref, w_ref, x_ref, gate_ref = refs[:6]
    fg_ref = refs[6] if final else None
    xo_ref = refs[-1]
    gt = gt_ref[...].astype(F32)
    h = jnp.concatenate([hf_ref[lt] + hb_ref[lt] for lt in range(hf_ref.shape[0])], axis=1)
    hg = (h * (gt * _sigmoid(gt))).astype(BF16)
    y = _dot(hg, w_ref[...])
    _residual_store(xo_ref, x_ref, y, gate_ref[...], fg_ref, nw, d, rows)


def _out_proj(kind, branch, gate_src, gate_col, extra, w_out, x, gate, final_g, *, col_major):
    b, t, d = x.shape
    wb = w_out.shape[0]
    tm, nw, view, xblock, xmap = _stream_tiling(t, d, col_major, 512)
    tile = lambda col: pl.BlockSpec((None, tm, wb), lambda bi, i: (bi, i, col))
    if kind == "gdn":
        in_specs = [tile(0), tile(0), tile(gate_col), pl.BlockSpec((1, HEAD_DIM), lambda bi, i: (0, 0))]
        args = [branch[0], branch[1], gate_src, extra.reshape(1, HEAD_DIM)]
        body = _gdn_out_kernel
    else:
        lane_tiles = pl.BlockSpec((None, wb // LANES, tm, LANES), lambda bi, i: (bi, 0, i, 0))
        in_specs = [lane_tiles, lane_tiles, tile(gate_col)]
        args = [branch[0], branch[1], gate_src]
        body = _lru_out_kernel
    in_specs += [pl.BlockSpec(w_out.shape, lambda bi, i: (0, 0)),
                 pl.BlockSpec(xblock, lambda bi, i: xmap(bi, i)),
                 pl.BlockSpec((None, 1, d), lambda bi, i: (bi, 0, 0))]
    args += [w_out, view(x), gate]
    final = final_g is not None
    if final:
        in_specs.append(pl.BlockSpec((1, d), lambda bi, i: (0, 0)))
        args.append(final_g.reshape(1, d))
    xv = view(x)
    out = pl.pallas_call(
        functools.partial(body, nw=nw, d=d, rows=tm // nw, final=final),
        grid=(b, t // tm),
        in_specs=in_specs,
        out_specs=pl.BlockSpec(xblock, lambda bi, i: xmap(bi, i)),
        out_shape=jax.ShapeDtypeStruct(xv.shape, F32),
        compiler_params=_cparams(("parallel", "parallel")),
        name=kind + "_out",
    )(*args)
    return out.reshape(b, t, d)


def _gdn_layer(streams, mods, norm_g, w_in, conv_w, a_log, dt_bias, head_g, w_out, *, col_major, final_g,
               update_ctx):
    qkvz_w = 2 * QK_HEADS * HEAD_DIM + 2 * V_HEADS * HEAD_DIM
    w_main = w_in[:, :qkvz_w].astype(BF16)
    w_ab = jnp.pad(w_in[:, qkvz_w:], ((0, 0), (0, LANES - (w_in.shape[1] - qkvz_w)))).astype(BF16)
    w_out16 = w_out.astype(BF16)
    state = None
    outs = []
    for (x, cm, fg, need_y), (shift, scale, gate) in zip(
            ((streams[0], False, None, update_ctx), (streams[1], col_major, final_g, True)), mods):
        b = x.shape[0]
        if state is None:
            state = jnp.zeros((b, QK_HEADS, 2, HEAD_DIM, V_HEADS // QK_HEADS * HEAD_DIM), F32)
        qn, kn, kt, v, z, cols, rows = _gdn_in(x, norm_g, shift, scale, w_main, w_ab, conv_w, a_log, dt_bias,
                                               col_major=cm)
        o_f, o_b, state = _gdn_scan(qn, kn, kt, v, cols, rows, state)
        if need_y:
            x = _out_proj("gdn", (o_f, o_b), z, 0, head_g, w_out16, x, gate, fg, col_major=cm)
        outs.append(x)
    return outs


def _lru_layer(streams, mods, norm_g, w_in, conv_w, conv_b, w_r, b_r, w_i, b_i, lam, w_out, *, col_major,
               final_g, update_ctx):
    w_in16 = w_in.astype(BF16)
    w_r16 = w_r.astype(BF16)
    w_i16 = w_i.astype(BF16)
    w_out16 = w_out.astype(BF16)
    state = None
    outs = []
    for (x, cm, fg, need_y), (shift, scale, gate) in zip(
            ((streams[0], False, None, update_ctx), (streams[1], col_major, final_g, True)), mods):
        b = x.shape[0]
        if state is None:
            state = jnp.zeros((b, 2, 1, conv_w.shape[1]), F32)
        proj = _in_proj(x, norm_g, shift, scale, w_in16, None, col_major=cm)
        h_f, h_b, state = _lru_scan(proj, conv_w, conv_b, w_r16, b_r, w_i16, b_i, lam, state)
        if need_y:
            x = _out_proj("lru", (h_f, h_b), proj, 1, None, w_out16, x, gate, fg, col_major=cm)
        outs.append(x)
    return outs


def kernel(x, c, ctx, c_ctx, mod_w, mod_b, norm_g, gdn_w_in, gdn_conv, gdn_a_log, gdn_dt_bias, gdn_norm_g,
           gdn_w_out, lru_w_in, lru_conv_w, lru_conv_b, lru_w_r, lru_b_r, lru_w_i, lru_b_i, lru_lambda,
           lru_w_out, final_g):
    b, _, d = x.shape
    depth = mod_w.shape[0]
    assert b + 1 <= MOD_ROWS
    cond = jnp.concatenate([c, c_ctx[None], jnp.zeros((MOD_ROWS - b - 1, d), F32)], axis=0)
    mod = _modulation(cond, mod_w, mod_b)
    for i in range(depth):
        lat = [mod[i, :b, k * d:(k + 1) * d].reshape(b, 1, d) for k in range(3)]
        cmod = [jnp.broadcast_to(mod[i, b, k * d:(k + 1) * d].reshape(1, 1, d), (b, 1, d)) for k in range(3)]
        col_major = (i + i // 2) % 2 == 1
        last = i == depth - 1
        j = i // 2
        common = dict(col_major=col_major, final_g=final_g if last else None, update_ctx=not last)
        if i % 2 == 0:
            ctx, x = _gdn_layer((ctx, x), (cmod, lat), norm_g[i], gdn_w_in[j], gdn_conv[j], gdn_a_log[j],
                                gdn_dt_bias[j], gdn_norm_g[j], gdn_w_out[j], **common)
        else:
            ctx, x = _lru_layer((ctx, x), (cmod, lat), norm_g[i], lru_w_in[j], lru_conv_w[j], lru_conv_b[j],
                                lru_w_r[j], lru_b_r[j], lru_w_i[j], lru_b_i[j], lru_lambda[j], lru_w_out[j],
                                **common)
    return x
```
